```python
import jax, jax.numpy as jnp
from jax import lax
import numpy as np

D_MODEL = 1024
BATCH = 4
SEQ = 4096
DEPTH = 2
DEC_BATCH = 32
DEC_SEQ = 32
PAST_LEN = 4096

CHUNK = 64
N_MIXERS = 2
N_GDN_LAYERS = (DEPTH + 1) // 2
N_SSD_LAYERS = DEPTH // 2
CONV_W = 4
EPS = 1e-6
GDN_HEADS = 8
GDN_DK = 128
GDN_DV = 128
GDN_QK = GDN_HEADS * GDN_DK
GDN_VD = GDN_HEADS * GDN_DV
GDN_CONV_DIM = 2 * GDN_QK + GDN_VD
GDN_IN = GDN_CONV_DIM + GDN_VD + 2 * GDN_HEADS
SSD_INNER = 2 * D_MODEL
SSD_HEADDIM = 64
SSD_HEADS = SSD_INNER // SSD_HEADDIM
SSD_GROUPS = 4
SSD_HPG = SSD_HEADS // SSD_GROUPS
SSD_DSTATE = 128
SSD_CONV_DIM = SSD_INNER + 2 * SSD_GROUPS * SSD_DSTATE
SSD_IN = SSD_INNER + SSD_CONV_DIM + SSD_HEADS
MOE_GROUPS = 4
MOE_PER_GROUP = 8
N_EXPERTS = MOE_GROUPS * MOE_PER_GROUP
MOE_TOPK = 2
D_EXPERT = 512
MOE_BLOCK = 128

kernel_name = 'hybrid_gdn_ssd_hmoe_stream_step'


def rmsnorm(x, g):
    xf = x.astype(jnp.float32)
    y = xf * lax.rsqrt(jnp.mean(xf * xf, axis=-1, keepdims=True) + EPS)
    return (y * g.astype(jnp.float32)).astype(x.dtype)


def l2norm(x):
    xf = x.astype(jnp.float32)
    return xf * lax.rsqrt(jnp.sum(xf * xf, axis=-1, keepdims=True) + EPS)


def causal_conv(x, buf, w, b=None):
    t = x.shape[1]
    xp = jnp.concatenate([buf.astype(x.dtype), x], axis=1)
    y = sum(xp[:, i:i + t] * w[i] for i in range(CONV_W))
    if b is not None:
        y = y + b
    return y, xp[:, t:]


def gated_delta_chunked(q, k, v, g, beta, S0):
    bn, t, nh, dk = q.shape
    dv = v.shape[-1]
    lc = min(CHUNK, t)
    nc = t // lc

    def blk(a):
        return jnp.moveaxis(a.reshape((bn, nc, lc, nh) + a.shape[3:]), 3, 2)

    q, k, v, g, beta = blk(q), blk(k), blk(v), blk(g), blk(beta)
    gam = jnp.cumsum(g, axis=-1)
    idx = jnp.arange(lc)
    incl = idx[:, None] >= idx[None, :]
    strict = idx[:, None] > idx[None, :]
    dec_incl = jnp.exp(jnp.where(incl, gam[..., :, None] - gam[..., None, :], -jnp.inf))
    dec_strict = jnp.where(strict, dec_incl, 0.0)
    A = beta[..., :, None] * dec_strict * jnp.einsum('bchld,bchsd->bchls', k, k)
    rhs = jnp.concatenate([beta[..., None] * v, (beta * jnp.exp(gam))[..., None] * k], axis=-1)
    X = lax.linalg.triangular_solve(A, rhs, left_side=True, lower=True, unit_diagonal=True)
    U, Wk = X[..., :dv], X[..., dv:]
    qk = jnp.einsum('bchld,bchsd->bchls', q, k) * dec_incl
    q_dec = q * jnp.exp(gam)[..., None]
    k_end = k * jnp.exp(gam[..., -1:] - gam)[..., None]
    chunk_dec = jnp.exp(gam[..., -1])

    def step(S, xs):
        u_c, wk_c, qk_c, qd_c, ke_c, cd_c = xs
        W = u_c - jnp.einsum('bhlk,bhkv->bhlv', wk_c, S)
        o = jnp.einsum('bhlk,bhkv->bhlv', qd_c, S) + jnp.einsum('bhls,bhsv->bhlv', qk_c, W)
        S = cd_c[..., None, None] * S + jnp.einsum('bhsk,bhsv->bhkv', ke_c, W)
        return S, o

    xs = tuple(jnp.moveaxis(a, 1, 0) for a in (U, Wk, qk, q_dec, k_end, chunk_dec))
    S, o = lax.scan(step, S0, xs)
    o = jnp.moveaxis(jnp.moveaxis(o, 0, 1), 2, 3).reshape(bn, t, nh, dv)
    return o, S


def gdn_mixer(h, conv_buf, S0, w_in, conv_w, A_log, dt_bias, norm_g, w_out):
    bn, t, _ = h.shape
    proj = h @ w_in
    qkv, conv_new = causal_conv(proj[..., :GDN_CONV_DIM], conv_buf, conv_w)
    qkv = jax.nn.silu(qkv)
    z = proj[..., GDN_CONV_DIM:GDN_CONV_DIM + GDN_VD].reshape(bn, t, GDN_HEADS, GDN_DV)
    b_raw = proj[..., GDN_CONV_DIM + GDN_VD:GDN_CONV_DIM + GDN_VD + GDN_HEADS].astype(jnp.float32)
    a_raw = proj[..., GDN_CONV_DIM + GDN_VD + GDN_HEADS:].astype(jnp.float32)
    q = l2norm(qkv[..., :GDN_QK].reshape(bn, t, GDN_HEADS, GDN_DK)) * (GDN_DK ** -0.5)
    k = l2norm(qkv[..., GDN_QK:2 * GDN_QK].reshape(bn, t, GDN_HEADS, GDN_DK))
    v = qkv[..., 2 * GDN_QK:].reshape(bn, t, GDN_HEADS, GDN_DV).astype(jnp.float32)
    beta = jax.nn.sigmoid(b_raw)
    g = -jnp.exp(A_log.astype(jnp.float32)) * jax.nn.softplus(a_raw + dt_bias.astype(jnp.float32))
    o, S = gated_delta_chunked(q, k, v, g, beta, S0.astype(jnp.float32))
    o = rmsnorm(o.astype(h.dtype), norm_g) * jax.nn.silu(z)
    out = o.reshape(bn, t, GDN_VD) @ w_out
    return out, conv_new, S.astype(h.dtype)


def ssd_chunked(x, dt, A, Bm, Cm, h0):
    bn, t, ng, ne, p = x.shape
    n = Bm.shape[-1]
    lc = min(CHUNK, t)
    nc = t // lc
    x = x.astype(jnp.float32).reshape(bn, nc, lc, ng, ne, p)
    dt = dt.reshape(bn, nc, lc, ng, ne)
    Bm = Bm.astype(jnp.float32).reshape(bn, nc, lc, ng, n)
    Cm = Cm.astype(jnp.float32).reshape(bn, nc, lc, ng, n)
    gam = jnp.cumsum(dt * A, axis=2)
    xdt = x * dt[..., None]
    gT = jnp.moveaxis(gam, 2, -1)
    idx = jnp.arange(lc)
    incl = idx[:, None] >= idx[None, :]
    L = jnp.exp(jnp.where(incl, gT[..., :, None] - gT[..., None, :], -jnp.inf))
    CB = jnp.einsum('bclgn,bcsgn->bcgls', Cm, Bm)
    y_intra = jnp.einsum('bcgels,bcsgep->bclgep', CB[:, :, :, None] * L, xdt)
    dec_end = jnp.exp(gam[:, :, -1:] - gam)
    states = jnp.einsum('bcsgn,bcsgep->bcgepn', Bm, xdt * dec_end[..., None])
    chunk_dec = jnp.exp(gam[:, :, -1])

    def step(hc, xs):
        st, cd = xs
        return hc * cd[..., None, None] + st, hc

    hT, h_prev = lax.scan(step, h0, (jnp.moveaxis(states, 1, 0), jnp.moveaxis(chunk_dec, 1, 0)))
    h_prev = jnp.moveaxis(h_prev, 0, 1)
    y_inter = jnp.einsum('bclgn,bcgepn->bclgep', Cm, h_prev) * jnp.exp(gam)[..., None]
    return (y_intra + y_inter).reshape(bn, t, ng, ne, p), hT


def ssd_mixer(h, conv_buf, h0, w_in, conv_w, conv_b, dt_bias, A_log, d_skip, norm_g, w_out):
    bn, t, _ = h.shape
    proj = h @ w_in
    z = proj[..., :SSD_INNER]
    xbc, conv_new = causal_conv(proj[..., SSD_INNER:SSD_INNER + SSD_CONV_DIM], conv_buf, conv_w, conv_b)
    xbc = jax.nn.silu(xbc)
    gn = SSD_GROUPS * SSD_DSTATE
    x = xbc[..., :SSD_INNER].reshape(bn, t, SSD_GROUPS, SSD_HPG, SSD_HEADDIM)
    Bm = xbc[..., SSD_INNER:SSD_INNER + gn].reshape(bn, t, SSD_GROUPS, SSD_DSTATE)
    Cm = xbc[..., SSD_INNER + gn:].reshape(bn, t, SSD_GROUPS, SSD_DSTATE)
    dt = jax.nn.softplus(proj[..., SSD_INNER + SSD_CONV_DIM:].astype(jnp.float32) + dt_bias.astype(jnp.float32))
    dt = dt.reshape(bn, t, SSD_GROUPS, SSD_HPG)
    A = -jnp.exp(A_log.astype(jnp.float32)).reshape(SSD_GROUPS, SSD_HPG)
    h0 = h0.astype(jnp.float32).reshape(bn, SSD_GROUPS, SSD_HPG, SSD_HEADDIM, SSD_DSTATE)
    y, hT = ssd_chunked(x, dt, A, Bm, Cm, h0)
    y = y + x.astype(jnp.float32) * d_skip.astype(jnp.float32).reshape(SSD_GROUPS, SSD_HPG)[..., None]
    y = y.reshape(bn, t, SSD_INNER).astype(h.dtype) * jax.nn.silu(z)
    gs = SSD_INNER // SSD_GROUPS
    y = rmsnorm(y.reshape(bn, t, SSD_GROUPS, gs), norm_g.reshape(SSD_GROUPS, gs)).reshape(bn, t, SSD_INNER)
    out = y @ w_out
    hT = hT.reshape(bn, SSD_HEADS, SSD_HEADDIM, SSD_DSTATE)
    return out, conv_new, hT.astype(h.dtype)


def grouped_experts(tok_x, eid, w_gate, w_up, w_down):
    n_tok, d = tok_x.shape
    n_asg = eid.shape[0]
    tok = jnp.arange(n_asg, dtype=jnp.int32) // MOE_TOPK
    order = jnp.argsort(eid)
    e_sorted = eid[order]
    counts = jnp.bincount(eid, length=N_EXPERTS)
    padded = (counts + MOE_BLOCK - 1) // MOE_BLOCK * MOE_BLOCK
    ends = jnp.cumsum(padded)
    start = jnp.cumsum(counts) - counts
    pstart = ends - padded
    rank = jnp.arange(n_asg, dtype=jnp.int32) - start[e_sorted]
    dest_sorted = (pstart[e_sorted] + rank).astype(jnp.int32)
    dest = jnp.zeros((n_asg,), jnp.int32).at[order].set(dest_sorted)
    n_blocks = (n_asg + N_EXPERTS * (MOE_BLOCK - 1) + MOE_BLOCK - 1) // MOE_BLOCK
    cap = n_blocks * MOE_BLOCK
    row_tok = jnp.full((cap,), n_tok, jnp.int32).at[dest].set(tok)
    x_pad = jnp.concatenate([tok_x, jnp.zeros((1, d), tok_x.dtype)], axis=0)
    xb = x_pad[row_tok].reshape(n_blocks, MOE_BLOCK, d)
    block_e = jnp.minimum(jnp.searchsorted(ends, jnp.arange(n_blocks) * MOE_BLOCK, side='right'), N_EXPERTS - 1)

    def expert_block(args):
        xi, e = args
        hid = jax.nn.silu(xi @ w_gate[e]) * (xi @ w_up[e])
        return hid @ w_down[e]

    yb = lax.map(expert_block, (xb, block_e)).reshape(cap, d)
    return yb[dest]


def hier_moe(h, w_grp, b_grp, w_exp, b_exp, w_gate, w_up, w_down):
    bn, t, d = h.shape
    tx = h.reshape(-1, d)
    n = tx.shape[0]
    grp_p = jax.nn.softmax((tx @ w_grp).astype(jnp.float32) + b_grp.astype(jnp.float32), axis=-1)
    gp, gi = lax.top_k(grp_p, 1)
    e_logits = ((tx @ w_exp).astype(jnp.float32) + b_exp.astype(jnp.float32)).reshape(n, MOE_GROUPS, MOE_PER_GROUP)
    sel = jnp.take_along_axis(e_logits, gi[:, :, None], axis=1)[:, 0]
    ev, ei = lax.top_k(jax.nn.softmax(sel, axis=-1), MOE_TOPK)
    weights = gp * ev / jnp.sum(ev, axis=-1, keepdims=True)
    eid = (gi * MOE_PER_GROUP + ei).reshape(-1).astype(jnp.int32)
    y = grouped_experts(tx, eid, w_gate, w_up, w_down).reshape(n, MOE_TOPK, d)
    y = jnp.sum(y * weights[..., None].astype(y.dtype), axis=1)
    return y.reshape(bn, t, d).astype(h.dtype)


def trunk(x, c, gdn_conv, gdn_ssm, ssd_conv, ssd_ssm, w):
    new_gc, new_gs, new_sc, new_ss = [], [], [], []
    for i in range(DEPTH):
        j = i // N_MIXERS
        mod = jax.nn.silu(c) @ w['ada_w'][i] + w['ada_b'][i]
        sh1, sc1, g1, sh2, sc2, g2 = jnp.split(mod[:, None, :], 6, axis=-1)
        h = rmsnorm(x, w['norm1_g'][i]) * (1 + sc1) + sh1
        if i % N_MIXERS == 0:
            out, cb, st = gdn_mixer(h, gdn_conv[j], gdn_ssm[j], w['gdn_w_in'][j], w['gdn_conv_w'][j],
                                    w['gdn_A_log'][j], w['gdn_dt_bias'][j], w['gdn_norm_g'][j], w['gdn_w_out'][j])
            new_gc.append(cb)
            new_gs.append(st)
        else:
            out, cb, st = ssd_mixer(h, ssd_conv[j], ssd_ssm[j], w['ssd_w_in'][j], w['ssd_conv_w'][j],
                                    w['ssd_conv_b'][j], w['ssd_dt_bias'][j], w['ssd_A_log'][j], w['ssd_D'][j],
                                    w['ssd_norm_g'][j], w['ssd_w_out'][j])
            new_sc.append(cb)
            new_ss.append(st)
        x = x + g1 * out
        h = rmsnorm(x, w['norm2_g'][i]) * (1 + sc2) + sh2
        x = x + g2 * hier_moe(h, w['moe_w_group'][i], w['moe_b_group'][i], w['moe_w_expert'][i],
                              w['moe_b_expert'][i], w['moe_w_gate'][i], w['moe_w_up'][i], w['moe_w_down'][i])
    y = rmsnorm(x, w['final_norm_g'])
    return y, jnp.stack(new_gc), jnp.stack(new_gs), jnp.stack(new_sc), jnp.stack(new_ss)


def _dt_bias(k, shape):
    dt = jnp.exp(jax.random.uniform(k, shape, jnp.float32) * (np.log(0.1) - np.log(1e-3)) + np.log(1e-3))
    return dt + jnp.log(-jnp.expm1(-dt))


def setup_inputs(seed: int = 0) -> dict:
    key = jax.random.key(seed)
    ks = iter(jax.random.split(key, 40))

    def nrm(shape, s):
        return jax.random.normal(next(ks), shape, jnp.float32) * s

    LA, LB, D = N_GDN_LAYERS, N_SSD_LAYERS, D_MODEL
    return {
        'x_prompt': nrm((BATCH, SEQ, D), 1.0),
        'x_sample': nrm((DEC_BATCH, DEC_SEQ, D), 1.0),
        'state_gdn_conv': nrm((LA, DEC_BATCH, CONV_W - 1, GDN_CONV_DIM), 1.0),
        'state_gdn_ssm': nrm((LA, DEC_BATCH, GDN_HEADS, GDN_DK, GDN_DV), 0.1),
        'state_ssd_conv': nrm((LB, DEC_BATCH, CONV_W - 1, SSD_CONV_DIM), 1.0),
        'state_ssd_ssm': nrm((LB, DEC_BATCH, SSD_HEADS, SSD_HEADDIM, SSD_DSTATE), 0.1),
        'c_prompt': nrm((BATCH, D), 1.0),
        'c_sample': nrm((DEC_BATCH, D), 1.0),
        'norm1_g': 1.0 + nrm((DEPTH, D), 0.02),
        'norm2_g': 1.0 + nrm((DEPTH, D), 0.02),
        'ada_w': nrm((DEPTH, D, 6 * D), 0.5 * D ** -0.5),
        'ada_b': nrm((DEPTH, 6 * D), 0.02),
        'gdn_w_in': nrm((LA, D, GDN_IN), D ** -0.5),
        'gdn_conv_w': nrm((LA, CONV_W, GDN_CONV_DIM), CONV_W ** -0.5),
        'gdn_A_log': jnp.log(jax.random.uniform(next(ks), (LA, GDN_HEADS), jnp.float32, 1.0, 16.0)),
        'gdn_dt_bias': _dt_bias(next(ks), (LA, GDN_HEADS)),
        'gdn_norm_g': 1.0 + nrm((LA, GDN_DV), 0.02),
        'gdn_w_out': nrm((LA, GDN_VD, D), GDN_VD ** -0.5),
        'ssd_w_in': nrm((LB, D, SSD_IN), D ** -0.5),
        'ssd_conv_w': nrm((LB, CONV_W, SSD_CONV_DIM), CONV_W ** -0.5),
        'ssd_conv_b': nrm((LB, SSD_CONV_DIM), 0.02),
        'ssd_dt_bias': _dt_bias(next(ks), (LB, SSD_HEADS)),
        'ssd_A_log': jnp.log(jax.random.uniform(next(ks), (LB, SSD_HEADS), jnp.float32, 1.0, 16.0)),
        'ssd_D': 1.0 + nrm((LB, SSD_HEADS), 0.1),
        'ssd_norm_g': 1.0 + nrm((LB, SSD_INNER), 0.02),
        'ssd_w_out': nrm((LB, SSD_INNER, D), SSD_INNER ** -0.5),
        'moe_w_group': nrm((DEPTH, D, MOE_GROUPS), D ** -0.5),
        'moe_b_group': nrm((DEPTH, MOE_GROUPS), 0.01),
        'moe_w_expert': nrm((DEPTH, D, N_EXPERTS), D ** -0.5),
        'moe_b_expert': nrm((DEPTH, N_EXPERTS), 0.01),
        'moe_w_gate': nrm((DEPTH, N_EXPERTS, D, D_EXPERT), D ** -0.5),
        'moe_w_up': nrm((DEPTH, N_EXPERTS, D, D_EXPERT), D ** -0.5),
        'moe_w_down': nrm((DEPTH, N_EXPERTS, D_EXPERT, D), D_EXPERT ** -0.5),
        'final_norm_g': 1.0 + nrm((D,), 0.02),
    }


def reference(x_prompt, x_sample, state_gdn_conv, state_gdn_ssm, state_ssd_conv, state_ssd_ssm,
              c_prompt, c_sample, norm1_g, norm2_g, ada_w, ada_b,
              gdn_w_in, gdn_conv_w, gdn_A_log, gdn_dt_bias, gdn_norm_g, gdn_w_out,
              ssd_w_in, ssd_conv_w, ssd_conv_b, ssd_dt_bias, ssd_A_log, ssd_D, ssd_norm_g, ssd_w_out,
              moe_w_group, moe_b_group, moe_w_expert, moe_b_expert, moe_w_gate, moe_w_up, moe_w_down,
              final_norm_g):
    w = {'norm1_g': norm1_g, 'norm2_g': norm2_g, 'ada_w': ada_w, 'ada_b': ada_b,
         'gdn_w_in': gdn_w_in, 'gdn_conv_w': gdn_conv_w, 'gdn_A_log': gdn_A_log, 'gdn_dt_bias': gdn_dt_bias,
         'gdn_norm_g': gdn_norm_g, 'gdn_w_out': gdn_w_out,
         'ssd_w_in': ssd_w_in, 'ssd_conv_w': ssd_conv_w, 'ssd_conv_b': ssd_conv_b, 'ssd_dt_bias': ssd_dt_bias,
         'ssd_A_log': ssd_A_log, 'ssd_D': ssd_D, 'ssd_norm_g': ssd_norm_g, 'ssd_w_out': ssd_w_out,
         'moe_w_group': moe_w_group, 'moe_b_group': moe_b_group, 'moe_w_expert': moe_w_expert,
         'moe_b_expert': moe_b_expert, 'moe_w_gate': moe_w_gate, 'moe_w_up': moe_w_up, 'moe_w_down': moe_w_down,
         'final_norm_g': final_norm_g}
    bp = x_prompt.shape[0]
    dt_ = x_prompt.dtype
    z_gc = jnp.zeros((N_GDN_LAYERS, bp, CONV_W - 1, GDN_CONV_DIM), dt_)
    z_gs = jnp.zeros((N_GDN_LAYERS, bp, GDN_HEADS, GDN_DK, GDN_DV), dt_)
    z_sc = jnp.zeros((N_SSD_LAYERS, bp, CONV_W - 1, SSD_CONV_DIM), dt_)
    z_ss = jnp.zeros((N_SSD_LAYERS, bp, SSD_HEADS, SSD_HEADDIM, SSD_DSTATE), dt_)
    y_prompt, p_gc, p_gs, p_sc, p_ss = trunk(x_prompt, c_prompt, z_gc, z_gs, z_sc, z_ss, w)
    y_sample, s_gc, s_gs, s_sc, s_ss = trunk(x_sample, c_sample, state_gdn_conv, state_gdn_ssm,
                                             state_ssd_conv, state_ssd_ssm, w)
    return (y_prompt, y_sample, p_gc, p_gs, p_sc, p_ss, s_gc, s_gs, s_sc, s_ss)
```

```python
import functools

import jax
import jax.numpy as jnp
from jax import lax
from jax.experimental import pallas as pl
from jax.experimental.pallas import tpu as pltpu

F32 = jnp.float32
BF16 = jnp.bfloat16
I32 = jnp.int32

D_MODEL = 1024
DEPTH = 2
CHUNK = 64
CONV_W = 4
EPS = 1e-6
GDN_HEADS = 8
GDN_DK = 128
GDN_DV = 128
GDN_QK = GDN_HEADS * GDN_DK
GDN_VD = GDN_HEADS * GDN_DV
GDN_CONV_DIM = 2 * GDN_QK + GDN_VD
SSD_INNER = 2 * D_MODEL
SSD_HEADDIM = 64
SSD_HEADS = SSD_INNER // SSD_HEADDIM
SSD_GROUPS = 4
SSD_HPG = SSD_HEADS // SSD_GROUPS
SSD_DSTATE = 128
SSD_GS = SSD_INNER // SSD_GROUPS
SSD_CONV_DIM = SSD_INNER + 2 * SSD_GROUPS * SSD_DSTATE
MOE_GROUPS = 4
MOE_PER_GROUP = 8
N_EXPERTS = MOE_GROUPS * MOE_PER_GROUP
MOE_TOPK = 2
D_EXPERT = 512

LANES = 128
SUBLANES = 8
VMEM_LIMIT = 48 * 1024 * 1024

ROW_TILE = 256
EXPERT_ROWS = 256
CONV_PAD = SUBLANES


def _cparams(sem):
    return pltpu.CompilerParams(dimension_semantics=sem, vmem_limit_bytes=VMEM_LIMIT)


def _mm(a, b):
    return jnp.dot(a.astype(BF16), b.astype(BF16), preferred_element_type=F32)


def _mm_nt(a, b):
    return lax.dot_general(a.astype(BF16), b.astype(BF16), (((1,), (1,)), ((), ())),
                           preferred_element_type=F32)


def _mm_tn(a, b):
    return lax.dot_general(a.astype(BF16), b.astype(BF16), (((0,), (0,)), ((), ())),
                           preferred_element_type=F32)


def _mm_f32(a, b):
    return jnp.dot(a, b, preferred_element_type=F32, precision=lax.Precision.HIGHEST)


def _silu(x):
    return x * jax.nn.sigmoid(x)


def _softplus(x):
    return jnp.maximum(x, 0.0) + jnp.log1p(jnp.exp(-jnp.abs(x)))


def _tiles(b, t):
    if t >= ROW_TILE:
        assert t % ROW_TILE == 0
        return 1, ROW_TILE
    assert ROW_TILE % t == 0 and b % (ROW_TILE // t) == 0
    return ROW_TILE // t, t


def _ada_kernel(c_ref, w_ref, b_ref, o_ref):
    o_ref[0] = _mm(_silu(c_ref[...]), w_ref[0]) + b_ref[0]


def _ada_mod(c_all, ada_w, ada_b):
    bp, d = c_all.shape
    n = ada_w.shape[-1]
    tn = 1024
    return pl.pallas_call(
        _ada_kernel,
        grid=(DEPTH, n // tn),
        in_specs=[pl.BlockSpec((bp, d), lambda l, j: (0, 0)),
                  pl.BlockSpec((1, d, tn), lambda l, j: (l, 0, j)),
                  pl.BlockSpec((1, 1, tn), lambda l, j: (l, 0, j))],
        out_specs=pl.BlockSpec((1, bp, tn), lambda l, j: (l, 0, j)),
        out_shape=jax.ShapeDtypeStruct((DEPTH, bp, n), F32),
        compiler_params=_cparams(("arbitrary", "arbitrary")),
        name="ada_mod",
    )(c_all, ada_w, ada_b.reshape(DEPTH, 1, n))


def _inproj_kernel(x_ref, g_ref, sc_ref, sh_ref, w_ref, o_ref):
    bb, tt, d = x_ref.shape
    x = x_ref[...]
    y = x * lax.rsqrt(jnp.mean(x * x, axis=-1, keepdims=True) + EPS) * g_ref[...]
    h = y * (1.0 + sc_ref[...]) + sh_ref[...]
    o = _mm(h.reshape(bb * tt, d), w_ref[...])
    o_ref[...] = o.reshape(bb, tt, o.shape[-1])


def _in_proj(x, norm_g, mod, sc_idx, sh_idx, w_bf16):
    b, t, d = x.shape
    n = w_bf16.shape[1]
    bb, tt = _tiles(b, t)
    return pl.pallas_call(
        _inproj_kernel,
        grid=(b // bb, t // tt),
        in_specs=[pl.BlockSpec((bb, tt, d), lambda i, j: (i, j, 0)),
                  pl.BlockSpec((1, d), lambda i, j: (0, 0)),
                  pl.BlockSpec((bb, 1, d), lambda i, j: (i, 0, sc_idx)),
                  pl.BlockSpec((bb, 1, d), lambda i, j: (i, 0, sh_idx)),
                  pl.BlockSpec((d, n), lambda i, j: (0, 0))],
        out_specs=pl.BlockSpec((bb, tt, n), lambda i, j: (i, j, 0)),
        out_shape=jax.ShapeDtypeStruct((b, t, n), F32),
        compiler_params=_cparams(("arbitrary", "arbitrary")),
        name="in_proj",
    )(x, norm_g.reshape(1, d), mod, mod, w_bf16)


def _conv_tile(xp_ref, x_tile, hist_ref, w_ref, first, bias=None):
    tt = x_tile.shape[0]
    h0 = CONV_PAD - (CONV_W - 1)

    @pl.when(first)
    def _():
        xp_ref[h0:CONV_PAD, :] = hist_ref[0]

    xp_ref[CONV_PAD:CONV_PAD + tt, :] = x_tile
    y = xp_ref[h0:h0 + tt, :] * w_ref[0:1, :]
    for i in range(1, CONV_W):
        y = y + xp_ref[h0 + i:h0 + i + tt, :] * w_ref[i:i + 1, :]
    xp_ref[h0:CONV_PAD, :] = xp_ref[tt + h0:tt + CONV_PAD, :]
    if bias is not None:
        y = y + bias
    return y


def _tri_masks(lc):
    r = lax.broadcasted_iota(I32, (lc, lc), 0)
    c = lax.broadcasted_iota(I32, (lc, lc), 1)
    return r >= c, r > c


def _unit_lower_inverse(a, lc):
    r = lax.broadcasted_iota(I32, (lc, lc), 0)
    c = lax.broadcasted_iota(I32, (lc, lc), 1)
    inv = jnp.where(r == c, 1.0, 0.0) - a
    p = a
    n = 2
    while n < lc:
        p = _mm_f32(p, p)
        inv = inv + _mm_f32(inv, p)
        n *= 2
    return inv


def _gdn_kernel(lc, qp_ref, kp_ref, vp_ref, z_ref, gt_ref, hq_ref, hk_ref, hv_ref,
                wq_ref, wk_ref, wv_ref, s0_ref, alog_ref, dtb_ref, ng_ref,
                o_ref, sout_ref,
                xq_s, xk_s, xv_s, q_s, k_s, v_s, beta_s, g_s, st_s):
    t = pl.program_id(1)
    tt = qp_ref.shape[1]
    first = t == 0

    @pl.when(first)
    def _():
        st_s[...] = s0_ref[0]

    q = _silu(_conv_tile(xq_s, qp_ref[0], hq_ref, wq_ref, first))
    k = _silu(_conv_tile(xk_s, kp_ref[0], hk_ref, wk_ref, first))
    v_s[...] = _silu(_conv_tile(xv_s, vp_ref[0], hv_ref, wv_ref, first))
    for h in range(GDN_HEADS):
        sl = slice(h * GDN_DK, (h + 1) * GDN_DK)
        qh = q[:, sl]
        kh = k[:, sl]
        q_s[:, sl] = qh * lax.rsqrt(jnp.sum(qh * qh, axis=-1, keepdims=True) + EPS) * (GDN_DK ** -0.5)
        k_s[:, sl] = kh * lax.rsqrt(jnp.sum(kh * kh, axis=-1, keepdims=True) + EPS)
    gates = gt_ref[0]
    beta_s[...] = jax.nn.sigmoid(gates)
    g_s[...] = -jnp.exp(alog_ref[...]) * _softplus(gates + dtb_ref[...])

    incl, strict = _tri_masks(lc)
    tri = jnp.where(incl, 1.0, 0.0)

    def chunk(c, carry):
        r0 = pl.multiple_of(c * lc, lc)
        rows = pl.ds(r0, lc)
        beta = beta_s[rows, :]
        gam = _mm_f32(tri, g_s[rows, :])
        gam_t = gam.T
        egam = jnp.exp(gam)
        glast = gam[lc - 1:lc, :]
        eend = jnp.exp(glast - gam)
        cdec = jnp.exp(glast)
        for h in range(GDN_HEADS):
            sl = slice(h * GDN_DK, (h + 1) * GDN_DK)
            gl = GDN_HEADS + h
            qh = q_s[rows, sl]
            kh = k_s[rows, sl]
            vh = v_s[rows, sl]
            b_col = beta[:, h:h + 1]
            diff = gam[:, gl:gl + 1] - gam_t[gl:gl + 1, :]
            dec_incl = jnp.exp(jnp.where(incl, diff, -jnp.inf))
            dec_strict = jnp.where(strict, dec_incl, 0.0)
            a = b_col * dec_strict * _mm_nt(kh, kh)
            tinv = _unit_lower_inverse(a, lc)
            eg = egam[:, gl:gl + 1]
            rhs = jnp.concatenate([b_col * vh, (b_col * eg) * kh], axis=-1)
            x = _mm(tinv, rhs)
            u = x[:, :GDN_DV]
            wk = x[:, GDN_DV:]
            qk = _mm_nt(qh, kh) * dec_incl
            s = st_s[h]
            w = u - _mm(wk, s)
            o = _mm(qh * eg, s) + _mm(qk, w)
            k_end = kh * eend[:, gl:gl + 1]
            st_s[h] = cdec[:, gl:gl + 1] * s + _mm_tn(k_end, w)
            on = o * lax.rsqrt(jnp.mean(o * o, axis=-1, keepdims=True) + EPS) * ng_ref[...]
            o_ref[0, rows, sl] = (on * _silu(z_ref[0, rows, sl])).astype(o_ref.dtype)
        return carry

    lax.fori_loop(0, tt // lc, chunk, 0)

    @pl.when(t == pl.num_programs(1) - 1)
    def _():
        sout_ref[0] = st_s[...]


def _gdn_core(proj, conv_hist, s0, conv_w, a_log, dt_bias, norm_g):
    b, t, _ = proj.shape
    lc = min(CHUNK, t)
    tt = min(ROW_TILE, t)
    nq = GDN_QK // GDN_QK
    lane_pad = jnp.zeros((LANES - 2 * GDN_HEADS,), F32)
    alog_row = jnp.concatenate([jnp.zeros((GDN_HEADS,), F32), a_log, lane_pad]).reshape(1, LANES)
    dtb_row = jnp.concatenate([jnp.zeros((GDN_HEADS,), F32), dt_bias, lane_pad]).reshape(1, LANES)
    gate_blk = (GDN_CONV_DIM + GDN_VD) // LANES
    col = lambda j: pl.BlockSpec((1, tt, GDN_QK), lambda i, s: (i, s, j))
    hist = lambda j: pl.BlockSpec((1, CONV_W - 1, GDN_QK), lambda i, s: (i, 0, j))
    cw = lambda j: pl.BlockSpec((CONV_W, GDN_QK), lambda i, s: (0, j))
    row = pl.BlockSpec((1, LANES), lambda i, s: (0, 0))
    state = pl.BlockSpec((1, GDN_HEADS, GDN_DK, GDN_DV), lambda i, s: (i, 0, 0, 0))
    del nq
    return pl.pallas_call(
        functools.partial(_gdn_kernel, lc),
        grid=(b, t // tt),
        in_specs=[col(0), col(1), col(2), col(3),
                  pl.BlockSpec((1, tt, LANES), lambda i, s: (i, s, gate_blk)),
                  hist(0), hist(1), hist(2), cw(0), cw(1), cw(2), state, row, row, row],
        out_specs=[pl.BlockSpec((1, tt, GDN_VD), lambda i, s: (i, s, 0)), state],
        out_shape=[jax.ShapeDtypeStruct((b, t, GDN_VD), BF16),
                   jax.ShapeDtypeStruct((b, GDN_HEADS, GDN_DK, GDN_DV), F32)],
        scratch_shapes=[pltpu.VMEM((CONV_PAD + tt, GDN_QK), F32)] * 3
        + [pltpu.VMEM((tt, GDN_QK), F32)] * 3
        + [pltpu.VMEM((tt, LANES), F32)] * 2
        + [pltpu.VMEM((GDN_HEADS, GDN_DK, GDN_DV), F32)],
        compiler_params=_cparams(("arbitrary", "arbitrary")),
        name="gdn_core",
    )(proj, proj, proj, proj, proj, conv_hist, conv_hist, conv_hist, conv_w, conv_w, conv_w,
      s0, alog_row, dtb_row, norm_g.reshape(1, GDN_DV))


def _ssd_kernel(lc, z_ref, xp_ref, bp_ref, cp_ref, dt_ref, hx_ref, hb_ref, hc_ref,
                wx_ref, wb_ref, wc_ref, bx_ref, bb_ref, bc_ref, h0_ref,
                alog_ref, dtb_ref, dskip_ref, ng_ref,
                y_ref, hout_ref,
                xx_s, xb_s, xc_s, x_s, b_s, c_s, dt_s, da_s, h_s):
    t = pl.program_id(2)
    tt = xp_ref.shape[1]
    first = t == 0
    p = SSD_HEADDIM

    @pl.when(first)
    def _():
        h_s[...] = h0_ref[0].reshape(SSD_HPG * p, SSD_DSTATE)

    x_s[...] = _silu(_conv_tile(xx_s, xp_ref[0], hx_ref, wx_ref, first, bx_ref[...]))
    b_s[...] = _silu(_conv_tile(xb_s, bp_ref[0], hb_ref, wb_ref, first, bb_ref[...]))
    c_s[...] = _silu(_conv_tile(xc_s, cp_ref[0], hc_ref, wc_ref, first, bc_ref[...]))
    dt = _softplus(dt_ref[0] + dtb_ref[0])
    dt_s[...] = dt
    da_s[...] = dt * (-jnp.exp(alog_ref[0]))

    incl, _ = _tri_masks(lc)
    tri = jnp.where(incl, 1.0, 0.0)
    lane = lax.broadcasted_iota(I32, (lc, LANES), 1)
    low_half = lane < p

    def chunk(c, carry):
        r0 = pl.multiple_of(c * lc, lc)
        rows = pl.ds(r0, lc)
        xc = x_s[rows, :]
        bc = b_s[rows, :]
        cc = c_s[rows, :]
        gam = _mm_f32(tri, da_s[rows, :])
        gam_t = gam.T
        glast = gam[lc - 1:lc, :]

        def per_column(m):
            cols = [jnp.broadcast_to(m[:, e:e + 1], (lc, LANES)) for e in range(SSD_HPG)]
            return jnp.concatenate([jnp.where(low_half, cols[2 * i], cols[2 * i + 1])
                                    for i in range(SSD_HPG // 2)], axis=-1)

        dt_x = per_column(dt_s[rows, :])
        eg_x = per_column(jnp.exp(gam))
        ee_x = per_column(jnp.exp(glast - gam))
        cdec = jnp.exp(glast)
        xdt = xc * dt_x
        cb = _mm_nt(cc, bc)
        h_prev = h_s[...]
        y_inter = _mm_nt(cc, h_prev) * eg_x
        st = _mm_tn(xdt * ee_x, bc)
        for e in range(SSD_HPG):
            hs = slice(e * p, (e + 1) * p)
            h_s[hs, :] = h_prev[hs, :] * cdec[:, e:e + 1] + st[hs, :]
        parts = []
        for pr in range(SSD_HPG // 2):
            cols = slice(pr * LANES, (pr + 1) * LANES)
            ys = []
            for e in (2 * pr, 2 * pr + 1):
                diff = gam[:, e:e + 1] - gam_t[e:e + 1, :]
                lm = jnp.exp(jnp.where(incl, diff, -jnp.inf))
                ys.append(_mm(cb * lm, xdt[:, cols]))
            parts.append(jnp.where(low_half, ys[0], ys[1]))
        y = jnp.concatenate(parts, axis=-1) + y_inter + xc * dskip_ref[0]
        y = y * _silu(z_ref[0, rows, :])
        y = y * lax.rsqrt(jnp.mean(y * y, axis=-1, keepdims=True) + EPS) * ng_ref[0]
        y_ref[0, rows, :] = y.astype(y_ref.dtype)
        return carry

    lax.fori_loop(0, tt // lc, chunk, 0)

    @pl.when(t == pl.num_programs(2) - 1)
    def _():
        hout_ref[0] = h_s[...].reshape(SSD_HPG, p, SSD_DSTATE)


def _ssd_core(proj, conv_hist, h0, conv_w, conv_b, dt_bias, a_log, d_skip, norm_g):
    b, t, _ = proj.shape
    lc = min(CHUNK, t)
    tt = min(ROW_TILE, t)
    g_n = SSD_GROUPS
    gs = SSD_GS
    xblk = SSD_INNER // gs
    bblk = 2 * SSD_INNER // LANES
    cblk = bblk + g_n
    dblk = cblk + g_n

    def per_group(v):
        return jnp.pad(v.reshape(g_n, 1, SSD_HPG), ((0, 0), (0, 0), (0, LANES - SSD_HPG)))

    dskip_x = jnp.repeat(d_skip, SSD_HEADDIM).reshape(g_n, 1, gs)
    grow = pl.BlockSpec((1, 1, LANES), lambda i, g, s: (g, 0, 0))
    cwb = conv_b.reshape(1, SSD_CONV_DIM)
    in_specs = [
        pl.BlockSpec((1, tt, gs), lambda i, g, s: (i, s, g)),
        pl.BlockSpec((1, tt, gs), lambda i, g, s: (i, s, xblk + g)),
        pl.BlockSpec((1, tt, LANES), lambda i, g, s: (i, s, bblk + g)),
        pl.BlockSpec((1, tt, LANES), lambda i, g, s: (i, s, cblk + g)),
        pl.BlockSpec((1, tt, LANES), lambda i, g, s: (i, s, dblk + g)),
        pl.BlockSpec((1, CONV_W - 1, gs), lambda i, g, s: (i, 0, g)),
        pl.BlockSpec((1, CONV_W - 1, LANES), lambda i, g, s: (i, 0, SSD_INNER // LANES + g)),
        pl.BlockSpec((1, CONV_W - 1, LANES), lambda i, g, s: (i, 0, SSD_INNER // LANES + g_n + g)),
        pl.BlockSpec((CONV_W, gs), lambda i, g, s: (0, g)),
        pl.BlockSpec((CONV_W, LANES), lambda i, g, s: (0, SSD_INNER // LANES + g)),
        pl.BlockSpec((CONV_W, LANES), lambda i, g, s: (0, SSD_INNER // LANES + g_n + g)),
        pl.BlockSpec((1, gs), lambda i, g, s: (0, g)),
        pl.BlockSpec((1, LANES), lambda i, g, s: (0, SSD_INNER // LANES + g)),
        pl.BlockSpec((1, LANES), lambda i, g, s: (0, SSD_INNER // LANES + g_n + g)),
        pl.BlockSpec((1, SSD_HPG, SSD_HEADDIM, SSD_DSTATE), lambda i, g, s: (i, g, 0, 0)),
        grow, grow,
        pl.BlockSpec((1, 1, gs), lambda i, g, s: (g, 0, 0)),
        pl.BlockSpec((1, 1, gs), lambda i, g, s: (g, 0, 0)),
    ]
    return pl.pallas_call(
        functools.partial(_ssd_kernel, lc),
        grid=(b, g_n, t // tt),
        in_specs=in_specs,
        out_specs=[pl.BlockSpec((1, tt, gs), lambda i, g, s: (i, s, g)),
                   pl.BlockSpec((1, SSD_HPG, SSD_HEADDIM, SSD_DSTATE), lambda i, g, s: (i, g, 0, 0))],
        out_shape=[jax.ShapeDtypeStruct((b, t, SSD_INNER), BF16),
                   jax.ShapeDtypeStruct((b, SSD_HEADS, SSD_HEADDIM, SSD_DSTATE), F32)],
        scratch_shapes=[pltpu.VMEM((CONV_PAD + tt, gs), F32),
                        pltpu.VMEM((CONV_PAD + tt, LANES), F32),
                        pltpu.VMEM((CONV_PAD + tt, LANES), F32),
                        pltpu.VMEM((tt, gs), F32),
                        pltpu.VMEM((tt, LANES), F32),
                        pltpu.VMEM((tt, LANES), F32),
                        pltpu.VMEM((tt, LANES), F32),
                        pltpu.VMEM((tt, LANES), F32),
                        pltpu.VMEM((SSD_HPG * SSD_HEADDIM, SSD_DSTATE), F32)],
        compiler_params=_cparams(("arbitrary", "arbitrary", "arbitrary")),
        name="ssd_core",
    )(proj, proj, proj, proj, proj, conv_hist, conv_hist, conv_hist, conv_w, conv_w, conv_w,
      cwb, cwb, cwb, h0, per_group(a_log), per_group(dt_bias), dskip_x,
      norm_g.reshape(g_n, 1, gs))


def _outproj_kernel(o_ref, x_ref, g1_ref, w_ref, n2_ref, sc_ref, sh_ref, wr_ref, br_ref,
                    x1_ref, h2_ref, ri_ref, rw_ref, cnt_ref, cnt_s):
    bb, tt, d = x_ref.shape
    tm = bb * tt
    step = pl.program_id(0) * pl.num_programs(1) + pl.program_id(1)

    @pl.when(step == 0)
    def _():
        cnt_s[...] = jnp.zeros_like(cnt_s)

    out = jnp.dot(o_ref[...].reshape(tm, o_ref.shape[-1]), w_ref[...], preferred_element_type=F32)
    x1 = x_ref[...] + g1_ref[...] * out.reshape(bb, tt, d)
    x1_ref[...] = x1
    y = x1 * lax.rsqrt(jnp.mean(x1 * x1, axis=-1, keepdims=True) + EPS) * n2_ref[...]
    h2 = (y * (1.0 + sc_ref[...]) + sh_ref[...]).reshape(tm, d)
    h2_ref[...] = h2

    logits = _mm_f32(h2, wr_ref[...]) + br_ref[...]
    lane = lax.broadcasted_iota(I32, (tm, LANES), 1)
    lane_f = lane.astype(F32)
    neg = -jnp.inf
    gl = jnp.where(lane < MOE_GROUPS, logits, neg)
    ge = jnp.exp(gl - jnp.max(gl, axis=-1, keepdims=True))
    grp_p = ge / jnp.sum(ge, axis=-1, keepdims=True)
    gp = jnp.max(grp_p, axis=-1, keepdims=True)
    gi = jnp.min(jnp.where(grp_p == gp, lane_f, float(LANES)), axis=-1, keepdims=True).astype(I32)
    lo = MOE_GROUPS + gi * MOE_PER_GROUP
    emask = (lane >= lo) & (lane < lo + MOE_PER_GROUP)
    sel = jnp.where(emask, logits, neg)
    se = jnp.exp(sel - jnp.max(sel, axis=-1, keepdims=True))
    p = jnp.where(emask, se / jnp.sum(se, axis=-1, keepdims=True), -1.0)
    v1 = jnp.max(p, axis=-1, keepdims=True)
    i1 = jnp.min(jnp.where(p == v1, lane_f, float(LANES)), axis=-1, keepdims=True).astype(I32)
    p2 = jnp.where(lane == i1, -1.0, p)
    v2 = jnp.max(p2, axis=-1, keepdims=True)
    i2 = jnp.min(jnp.where(p2 == v2, lane_f, float(LANES)), axis=-1, keepdims=True).astype(I32)
    den = v1 + v2
    w1 = gp * v1 / den
    w2 = gp * v2 / den
    e1 = i1 - MOE_GROUPS
    e2 = i2 - MOE_GROUPS

    hit1 = lane == e1
    hit2 = lane == e2
    onehot = jnp.where(hit1 | hit2, 1.0, 0.0)
    r = lax.broadcasted_iota(I32, (tm, tm), 0)
    c = lax.broadcasted_iota(I32, (tm, tm), 1)
    before = _mm(jnp.where(r > c, 1.0, 0.0), onehot) + cnt_s[...]
    r1 = jnp.sum(jnp.where(hit1, before, 0.0), axis=-1, keepdims=True).astype(I32)
    r2 = jnp.sum(jnp.where(hit2, before, 0.0), axis=-1, keepdims=True).astype(I32)
    cnt_s[...] = cnt_s[...] + jnp.sum(onehot, axis=0, keepdims=True)
    cnt_ref[...] = cnt_s[...].astype(I32)
    ri_ref[...] = jnp.where(lane == 0, e1, jnp.where(lane == 1, e2, jnp.where(lane == 2, r1, jnp.where(lane == 3, r2, 0))))
    rw_ref[...] = jnp.where(lane == 0, w1, jnp.where(lane == 1, w2, 0.0))


def _out_proj_route(o, x, mod, w_bf16, norm2_g, w_router, b_router):
    b, t, d = x.shape
    kdim = o.shape[-1]
    bb, tt = _tiles(b, t)
    tm = bb * tt
    nt = t // tt
    tok = lambda i, j: (i * nt + j, 0)
    modspec = lambda idx: pl.BlockSpec((bb, 1, d), lambda i, j: (i, 0, idx))
    return pl.pallas_call(
        _outproj_kernel,
        grid=(b // bb, nt),
        in_specs=[pl.BlockSpec((bb, tt, kdim), lambda i, j: (i, j, 0)),
                  pl.BlockSpec((bb, tt, d), lambda i, j: (i, j, 0)),
                  modspec(2),
                  pl.BlockSpec((kdim, d), lambda i, j: (0, 0)),
                  pl.BlockSpec((1, d), lambda i, j: (0, 0)),
                  modspec(4), modspec(3),
                  pl.BlockSpec((d, LANES), lambda i, j: (0, 0)),
                  pl.BlockSpec((1, LANES), lambda i, j: (0, 0))],
        out_specs=[pl.BlockSpec((bb, tt, d), lambda i, j: (i, j, 0)),
                   pl.BlockSpec((tm, d), tok),
                   pl.BlockSpec((tm, LANES), tok),
                   pl.BlockSpec((tm, LANES), tok),
                   pl.BlockSpec((1, LANES), lambda i, j: (0, 0))],
        out_shape=[jax.ShapeDtypeStruct((b, t, d), F32),
                   jax.ShapeDtypeStruct((b * t, d), F32),
                   jax.ShapeDtypeStruct((b * t, LANES), I32),
                   jax.ShapeDtypeStruct((b * t, LANES), F32),
                   jax.ShapeDtypeStruct((1, LANES), I32)],
        scratch_shapes=[pltpu.VMEM((1, LANES), F32)],
        compiler_params=_cparams(("arbitrary", "arbitrary")),
        name="out_proj_route",
    )(o, x, mod, w_bf16, norm2_g.reshape(1, d), mod, mod, w_router, b_router)


def _dispatch_kernel(dest_ref, h2_ref, xb_in_ref, xb_ref, sem):
    del xb_in_ref
    tm = h2_ref.shape[0]

    def copy(r, k):
        return pltpu.make_async_copy(h2_ref.at[pl.ds(r, 1), :],
                                     xb_ref.at[pl.ds(dest_ref[0, 0, MOE_TOPK * r + k], 1), :], sem)

    def start(r, carry):
        for k in range(MOE_TOPK):
            copy(r, k).start()
        return carry

    def wait(r, carry):
        for k in range(MOE_TOPK):
            copy(r, k).wait()
        return carry

    lax.fori_loop(0, tm, start, 0)
    lax.fori_loop(0, tm, wait, 0)


def _dispatch(h2, dest_tiles, cap):
    n, d = h2.shape
    nt, _, per = dest_tiles.shape
    tm = per // MOE_TOPK
    return pl.pallas_call(
        _dispatch_kernel,
        grid=(nt,),
        in_specs=[pl.BlockSpec((1, 1, per), lambda i: (i, 0, 0), memory_space=pltpu.SMEM),
                  pl.BlockSpec((tm, d), lambda i: (i, 0)),
                  pl.BlockSpec(memory_space=pl.ANY)],
        out_specs=pl.BlockSpec(memory_space=pl.ANY),
        out_shape=jax.ShapeDtypeStruct((cap, d), F32),
        scratch_shapes=[pltpu.SemaphoreType.DMA],
        input_output_aliases={2: 0},
        compiler_params=_cparams(("arbitrary",)),
        name="moe_dispatch",
    )(dest_tiles, h2, jnp.zeros((cap, d), F32))


def _expert_kernel(be_ref, nu_ref, x_ref, wg_ref, wu_ref, wd_ref, y_ref):
    used = pl.program_id(0) < nu_ref[0]

    @pl.when(used)
    def _():
        x = x_ref[...]
        hid = _silu(_mm(x, wg_ref[0])) * _mm(x, wu_ref[0])
        y_ref[...] = _mm(hid, wd_ref[0])

    @pl.when(jnp.logical_not(used))
    def _():
        y_ref[...] = jnp.zeros_like(y_ref)


def _experts(xb, block_e, n_used, w_gate, w_up, w_down):
    cap, d = xb.shape
    nb = cap // EXPERT_ROWS
    blk = lambda i, be, nu: (jnp.minimum(i, nu[0] - 1), 0)
    wsel = lambda i, be, nu: (be[jnp.minimum(i, nu[0] - 1)], 0, 0)
    return pl.pallas_call(
        _expert_kernel,
        grid_spec=pltpu.PrefetchScalarGridSpec(
            num_scalar_prefetch=2,
            grid=(nb,),
            in_specs=[pl.BlockSpec((EXPERT_ROWS, d), blk),
                      pl.BlockSpec((1, d, D_EXPERT), wsel),
                      pl.BlockSpec((1, d, D_EXPERT), wsel),
                      pl.BlockSpec((1, D_EXPERT, d), wsel)],
            out_specs=pl.BlockSpec((EXPERT_ROWS, d), lambda i, be, nu: (i, 0))),
        out_shape=jax.ShapeDtypeStruct((cap, d), F32),
        compiler_params=_cparams(("arbitrary",)),
        name="moe_experts",
    )(block_e, n_used, xb, w_gate, w_up, w_down)


def _combine_kernel(final, dest_ref, x1_ref, rw_ref, g2_ref, fg_ref, yb_ref, o_ref, buf_s, sem):
    bb, tt, d = x1_ref.shape
    tm = bb * tt

    def copy(r, k):
        return pltpu.make_async_copy(yb_ref.at[pl.ds(dest_ref[0, 0, MOE_TOPK * r + k], 1), :],
                                     buf_s.at[k, pl.ds(r, 1), :], sem)

    def start(r, carry):
        for k in range(MOE_TOPK):
            copy(r, k).start()
        return carry

    def wait(r, carry):
        for k in range(MOE_TOPK):
            copy(r, k).wait()
        return carry

    lax.fori_loop(0, tm, start, 0)
    lax.fori_loop(0, tm, wait, 0)
    rw = rw_ref[...]
    moe = buf_s[0] * rw[:, 0:1] + buf_s[1] * rw[:, 1:2]
    x2 = x1_ref[...] + g2_ref[...] * moe.reshape(bb, tt, d)
    if final:
        x2 = x2 * lax.rsqrt(jnp.mean(x2 * x2, axis=-1, keepdims=True) + EPS) * fg_ref[...]
    o_ref[...] = x2


def _combine(x1, route_w, mod, yb, dest_tiles, final_g, final):
    b, t, d = x1.shape
    bb, tt = _tiles(b, t)
    tm = bb * tt
    nt = t // tt
    per = dest_tiles.shape[-1]
    return pl.pallas_call(
        functools.partial(_combine_kernel, final),
        grid=(b // bb, nt),
        in_specs=[pl.BlockSpec((1, 1, per), lambda i, j: (i * nt + j, 0, 0), memory_space=pltpu.SMEM),
                  pl.BlockSpec((bb, tt, d), lambda i, j: (i, j, 0)),
                  pl.BlockSpec((tm, LANES), lambda i, j: (i * nt + j, 0)),
                  pl.BlockSpec((bb, 1, d), lambda i, j: (i, 0, 5)),
                  pl.BlockSpec((1, d), lambda i, j: (0, 0)),
                  pl.BlockSpec(memory_space=pl.ANY)],
        out_specs=pl.BlockSpec((bb, tt, d), lambda i, j: (i, j, 0)),
        out_shape=jax.ShapeDtypeStruct((b, t, d), F32),
        scratch_shapes=[pltpu.VMEM((MOE_TOPK, tm, d), F32), pltpu.SemaphoreType.DMA],
        compiler_params=_cparams(("arbitrary", "arbitrary")),
        name="moe_combine",
    )(dest_tiles, x1, route_w, mod, final_g.reshape(1, d), yb)


def _moe(x1, h2, route_i, route_w, counts, mod, w_gate, w_up, w_down, final_g, final):
    b, t, d = x1.shape
    n = b * t
    bb, tt = _tiles(b, t)
    tm = bb * tt
    n_asg = n * MOE_TOPK
    nb = (n_asg + N_EXPERTS * (EXPERT_ROWS - 1) + EXPERT_ROWS - 1) // EXPERT_ROWS
    cap = nb * EXPERT_ROWS
    cnt = counts[0, :N_EXPERTS]
    padded = (cnt + EXPERT_ROWS - 1) // EXPERT_ROWS * EXPERT_ROWS
    ends = jnp.cumsum(padded)
    pstart = ends - padded
    eid = route_i[:, :MOE_TOPK]
    rank = route_i[:, MOE_TOPK:2 * MOE_TOPK]
    first_slot = jnp.sum(jnp.where(eid[..., None] == jnp.arange(N_EXPERTS, dtype=I32), pstart, 0), axis=-1)
    dest_tiles = (first_slot + rank).astype(I32).reshape(n // tm, 1, tm * MOE_TOPK)
    blk_start = jnp.arange(nb, dtype=I32) * EXPERT_ROWS
    block_e = jnp.minimum(jnp.sum(blk_start[:, None] >= ends[None, :], axis=-1), N_EXPERTS - 1).astype(I32)
    n_used = (ends[-1:] // EXPERT_ROWS).astype(I32)
    xb = _dispatch(h2, dest_tiles, cap)
    yb = _experts(xb, block_e, n_used, w_gate, w_up, w_down)
    return _combine(x1, route_w, mod, yb, dest_tiles, final_g, final)


def _pad_cols(w, n):
    return jnp.pad(w, ((0, 0), (0, n - w.shape[1])))


def _trunk(x, mods, gdn_conv, gdn_ssm, ssd_conv, ssd_ssm, w):
    b, t, d = x.shape
    gdn_main = GDN_CONV_DIM + GDN_VD
    w_in = jnp.concatenate([w['gdn_w_in'][0][:, :gdn_main], _pad_cols(w['gdn_w_in'][0][:, gdn_main:], LANES)],
                           axis=1).astype(BF16)
    proj = _in_proj(x, w['norm1_g'][0], mods[0], 1, 0, w_in)
    o, gdn_state = _gdn_core(proj, gdn_conv[0], gdn_ssm[0], w['gdn_conv_w'][0], w['gdn_A_log'][0],
                             w['gdn_dt_bias'][0], w['gdn_norm_g'][0])
    gdn_hist = proj[:, t - (CONV_W - 1):, :GDN_CONV_DIM]
    x = _layer_tail(0, o, x, mods[0], w['gdn_w_out'][0], w, False)
    ssd_main = SSD_INNER + SSD_CONV_DIM
    dt_cols = [_pad_cols(w['ssd_w_in'][0][:, ssd_main + g * SSD_HPG: ssd_main + (g + 1) * SSD_HPG], LANES)
               for g in range(SSD_GROUPS)]
    w_in = jnp.concatenate([w['ssd_w_in'][0][:, :ssd_main]] + dt_cols, axis=1).astype(BF16)
    proj = _in_proj(x, w['norm1_g'][1], mods[1], 1, 0, w_in)
    y, ssd_state = _ssd_core(proj, ssd_conv[0], ssd_ssm[0], w['ssd_conv_w'][0], w['ssd_conv_b'][0],
                             w['ssd_dt_bias'][0], w['ssd_A_log'][0], w['ssd_D'][0], w['ssd_norm_g'][0])
    ssd_hist = proj[:, t - (CONV_W - 1):, SSD_INNER:ssd_main]
    y_out = _layer_tail(1, y, x, mods[1], w['ssd_w_out'][0], w, True)
    return y_out, gdn_hist[None], gdn_state[None], ssd_hist[None], ssd_state[None]


def _layer_tail(i, mixed, x, mod, w_out, w, final):
    d = x.shape[-1]
    w_router = _pad_cols(jnp.concatenate([w['moe_w_group'][i], w['moe_w_expert'][i]], axis=1), LANES)
    b_router = _pad_cols(jnp.concatenate([w['moe_b_group'][i], w['moe_b_expert'][i]]).reshape(1, -1), LANES)
    x1, h2, route_i, route_w, counts = _out_proj_route(mixed, x, mod, w_out.astype(BF16), w['norm2_g'][i],
                                                       w_router, b_router)
    del d
    return _moe(x1, h2, route_i, route_w, counts, mod, w['moe_w_gate'][i], w['moe_w_up'][i], w['moe_w_down'][i],
                w['final_norm_g'], final)


def kernel(x_prompt, x_sample, state_gdn_conv, state_gdn_ssm, state_ssd_conv, state_ssd_ssm, c_prompt, c_sample,
           norm1_g, norm2_g, ada_w, ada_b, gdn_w_in, gdn_conv_w, gdn_A_log, gdn_dt_bias, gdn_norm_g, gdn_w_out,
           ssd_w_in, ssd_conv_w, ssd_conv_b, ssd_dt_bias, ssd_A_log, ssd_D, ssd_norm_g, ssd_w_out,
           moe_w_group, moe_b_group, moe_w_expert, moe_b_expert, moe_w_gate, moe_w_up, moe_w_down, final_norm_g):
    w = {'norm1_g': norm1_g, 'norm2_g': norm2_g, 'gdn_w_in': gdn_w_in, 'gdn_conv_w': gdn_conv_w,
         'gdn_A_log': gdn_A_log, 'gdn_dt_bias': gdn_dt_bias, 'gdn_norm_g': gdn_norm_g, 'gdn_w_out': gdn_w_out,
         'ssd_w_in': ssd_w_in, 'ssd_conv_w': ssd_conv_w, 'ssd_conv_b': ssd_conv_b, 'ssd_dt_bias': ssd_dt_bias,
         'ssd_A_log': ssd_A_log, 'ssd_D': ssd_D, 'ssd_norm_g': ssd_norm_g, 'ssd_w_out': ssd_w_out,
         'moe_w_group': moe_w_group, 'moe_b_group': moe_b_group, 'moe_w_expert': moe_w_expert,
         'moe_b_expert': moe_b_expert, 'moe_w_gate': moe_w_gate, 'moe_w_up': moe_w_up, 'moe_w_down': moe_w_down,
         'final_norm_g': final_norm_g}
    bp = x_prompt.shape[0]
    bs = x_sample.shape[0]
    dt_ = x_prompt.dtype
    n_seq = bp + bs
    n_pad = -n_seq % SUBLANES
    c_all = jnp.concatenate([c_prompt, c_sample, jnp.zeros((n_pad, c_prompt.shape[1]), dt_)], axis=0)
    mod_all = _ada_mod(c_all, ada_w, ada_b)
    mods_p = [mod_all[l, :bp][:, None, :] for l in range(DEPTH)]
    mods_s = [mod_all[l, bp:n_seq][:, None, :] for l in range(DEPTH)]
    n_gdn = state_gdn_conv.shape[0]
    n_ssd = state_ssd_conv.shape[0]
    z_gc = jnp.zeros((n_gdn, bp) + state_gdn_conv.shape[2:], dt_)
    z_gs = jnp.zeros((n_gdn, bp) + state_gdn_ssm.shape[2:], dt_)
    z_sc = jnp.zeros((n_ssd, bp) + state_ssd_conv.shape[2:], dt_)
    z_ss = jnp.zeros((n_ssd, bp) + state_ssd_ssm.shape[2:], dt_)
    y_p, p_gc, p_gs, p_sc, p_ss = _trunk(x_prompt, mods_p, z_gc, z_gs, z_sc, z_ss, w)
    y_s, s_gc, s_gs, s_sc, s_ss = _trunk(x_sample, mods_s, state_gdn_conv, state_gdn_ssm,
                                         state_ssd_conv, state_ssd_ssm, w)
    return (y_p, y_s, p_gc, p_gs, p_sc, p_ss, s_gc, s_gs, s_sc, s_ss)
```

```python
import functools

import jax
import jax.numpy as jnp
from jax import lax
from jax.experimental import pallas as pl
from jax.experimental.pallas import tpu as pltpu

F32 = jnp.float32
BF16 = jnp.bfloat16
I32 = jnp.int32

D_MODEL = 1024
DEPTH = 2
CHUNK = 64
CONV_W = 4
EPS = 1e-6
GDN_HEADS = 8
GDN_DK = 128
GDN_DV = 128
GDN_QK = GDN_HEADS * GDN_DK
GDN_VD = GDN_HEADS * GDN_DV
GDN_CONV_DIM = 2 * GDN_QK + GDN_VD
SSD_INNER = 2 * D_MODEL
SSD_HEADDIM = 64
SSD_HEADS = SSD_INNER // SSD_HEADDIM
SSD_GROUPS = 4
SSD_HPG = SSD_HEADS // SSD_GROUPS
SSD_DSTATE = 128
SSD_GS = SSD_INNER // SSD_GROUPS
SSD_CONV_DIM = SSD_INNER + 2 * SSD_GROUPS * SSD_DSTATE
MOE_GROUPS = 4
MOE_PER_GROUP = 8
N_EXPERTS = MOE_GROUPS * MOE_PER_GROUP
MOE_TOPK = 2
D_EXPERT = 512

LANES = 128
SUBLANES = 8
VMEM_LIMIT = 48 * 1024 * 1024

ROW_TILE = 256
EXPERT_ROWS = 256
CONV_PAD = SUBLANES


def _cparams(sem):
    return pltpu.CompilerParams(dimension_semantics=sem, vmem_limit_bytes=VMEM_LIMIT)


def _mm(a, b):
    return jnp.dot(a.astype(BF16), b.astype(BF16), preferred_element_type=F32)


def _mm_nt(a, b):
    return lax.dot_general(a.astype(BF16), b.astype(BF16), (((1,), (1,)), ((), ())),
                           preferred_element_type=F32)


def _mm_tn(a, b):
    return lax.dot_general(a.astype(BF16), b.astype(BF16), (((0,), (0,)), ((), ())),
                           preferred_element_type=F32)


def _mm_f32(a, b):
    return jnp.dot(a, b, preferred_element_type=F32, precision=lax.Precision.HIGHEST)


def _silu(x):
    return x * jax.nn.sigmoid(x)


def _softplus(x):
    return jnp.maximum(x, 0.0) + jnp.log1p(jnp.exp(-jnp.abs(x)))


def _tiles(b, t):
    if t >= ROW_TILE:
        assert t % ROW_TILE == 0
        return 1, ROW_TILE
    assert ROW_TILE % t == 0 and b % (ROW_TILE // t) == 0
    return ROW_TILE // t, t


def _ada_kernel(c_ref, w_ref, b_ref, o_ref):
    o_ref[0] = _mm(_silu(c_ref[...]), w_ref[0]) + b_ref[0]


def _ada_mod(c_all, ada_w, ada_b):
    bp, d = c_all.shape
    n = ada_w.shape[-1]
    tn = 1024
    return pl.pallas_call(
        _ada_kernel,
        grid=(DEPTH, n // tn),
        in_specs=[pl.BlockSpec((bp, d), lambda l, j: (0, 0)),
                  pl.BlockSpec((1, d, tn), lambda l, j: (l, 0, j)),
                  pl.BlockSpec((1, 1, tn), lambda l, j: (l, 0, j))],
        out_specs=pl.BlockSpec((1, bp, tn), lambda l, j: (l, 0, j)),
        out_shape=jax.ShapeDtypeStruct((DEPTH, bp, n), F32),
        compiler_params=_cparams(("arbitrary", "arbitrary")),
        name="ada_mod",
    )(c_all, ada_w, ada_b.reshape(DEPTH, 1, n))


def _inproj_kernel(x_ref, g_ref, sc_ref, sh_ref, w_ref, o_ref):
    bb, tt, d = x_ref.shape
    x = x_ref[...]
    y = x * lax.rsqrt(jnp.mean(x * x, axis=-1, keepdims=True) + EPS) * g_ref[...]
    h = y * (1.0 + sc_ref[...]) + sh_ref[...]
    o = _mm(h.reshape(bb * tt, d), w_ref[...])
    o_ref[...] = o.reshape(bb, tt, o.shape[-1])


def _in_proj(x, norm_g, mod, sc_idx, sh_idx, w_bf16):
    b, t, d = x.shape
    n = w_bf16.shape[1]
    bb, tt = _tiles(b, t)
    return pl.pallas_call(
        _inproj_kernel,
        grid=(b // bb, t // tt),
        in_specs=[pl.BlockSpec((bb, tt, d), lambda i, j: (i, j, 0)),
                  pl.BlockSpec((1, d), lambda i, j: (0, 0)),
                  pl.BlockSpec((bb, 1, d), lambda i, j: (i, 0, sc_idx)),
                  pl.BlockSpec((bb, 1, d), lambda i, j: (i, 0, sh_idx)),
                  pl.BlockSpec((d, n), lambda i, j: (0, 0))],
        out_specs=pl.BlockSpec((bb, tt, n), lambda i, j: (i, j, 0)),
        out_shape=jax.ShapeDtypeStruct((b, t, n), F32),
        compiler_params=_cparams(("arbitrary", "arbitrary")),
        name="in_proj",
    )(x, norm_g.reshape(1, d), mod, mod, w_bf16)


def _conv_tile(xp_ref, x_tile, hist_ref, w_ref, first, bias=None):
    tt = x_tile.shape[0]
    h0 = CONV_PAD - (CONV_W - 1)

    @pl.when(first)
    def _():
        xp_ref[h0:CONV_PAD, :] = hist_ref[0]

    xp_ref[CONV_PAD:CONV_PAD + tt, :] = x_tile
    y = xp_ref[h0:h0 + tt, :] * w_ref[0:1, :]
    for i in range(1, CONV_W):
        y = y + xp_ref[h0 + i:h0 + i + tt, :] * w_ref[i:i + 1, :]
    xp_ref[h0:CONV_PAD, :] = xp_ref[tt + h0:tt + CONV_PAD, :]
    if bias is not None:
        y = y + bias
    return y


def _tri_masks(lc):
    r = lax.broadcasted_iota(I32, (lc, lc), 0)
    c = lax.broadcasted_iota(I32, (lc, lc), 1)
    return r >= c, r > c


def _split_bf16(a):
    hi = a.astype(BF16)
    return hi, (a - hi.astype(F32)).astype(BF16)


def _mm_split(a, b):
    ah, al = _split_bf16(a)
    bh, bl = _split_bf16(b)
    dot = functools.partial(jnp.dot, preferred_element_type=F32)
    return dot(ah, bh) + dot(ah, bl) + dot(al, bh)


def _unit_lower_inverses(mats, lc):
    r = lax.broadcasted_iota(I32, (lc, lc), 0)
    c = lax.broadcasted_iota(I32, (lc, lc), 1)
    eye = jnp.where(r == c, 1.0, 0.0)
    invs = [eye - a for a in mats]
    pows = list(mats)
    n = 2
    while n < lc:
        pows = [_mm_split(p, p) for p in pows]
        invs = [inv + _mm_split(inv, p) for inv, p in zip(invs, pows)]
        n *= 2
    return invs


def _gdn_kernel(lc, qp_ref, kp_ref, vp_ref, z_ref, gt_ref, hq_ref, hk_ref, hv_ref,
                wq_ref, wk_ref, wv_ref, s0_ref, alog_ref, dtb_ref, ng_ref,
                o_ref, sout_ref,
                xq_s, xk_s, xv_s, q_s, k_s, v_s, beta_s, g_s, st_s):
    t = pl.program_id(1)
    tt = qp_ref.shape[1]
    first = t == 0

    @pl.when(first)
    def _():
        st_s[...] = s0_ref[0]

    q = _silu(_conv_tile(xq_s, qp_ref[0], hq_ref, wq_ref, first))
    k = _silu(_conv_tile(xk_s, kp_ref[0], hk_ref, wk_ref, first))
    v_s[...] = _silu(_conv_tile(xv_s, vp_ref[0], hv_ref, wv_ref, first))
    for h in range(GDN_HEADS):
        sl = slice(h * GDN_DK, (h + 1) * GDN_DK)
        qh = q[:, sl]
        kh = k[:, sl]
        q_s[:, sl] = qh * lax.rsqrt(jnp.sum(qh * qh, axis=-1, keepdims=True) + EPS) * (GDN_DK ** -0.5)
        k_s[:, sl] = kh * lax.rsqrt(jnp.sum(kh * kh, axis=-1, keepdims=True) + EPS)
    gates = gt_ref[0]
    beta_s[...] = jax.nn.sigmoid(gates)
    g_s[...] = -jnp.exp(alog_ref[...]) * _softplus(gates + dtb_ref[...])

    incl, strict = _tri_masks(lc)
    tri = jnp.where(incl, 1.0, 0.0)

    def chunk(c, carry):
        r0 = pl.multiple_of(c * lc, lc)
        rows = pl.ds(r0, lc)
        beta = beta_s[rows, :]
        gam = _mm_f32(tri, g_s[rows, :])
        gam_t = gam.T
        egam = jnp.exp(gam)
        glast = gam[lc - 1:lc, :]
        eend = jnp.exp(glast - gam)
        cdec = jnp.exp(glast)
        heads = range(GDN_HEADS)
        sls = [slice(h * GDN_DK, (h + 1) * GDN_DK) for h in heads]
        qs = [q_s[rows, sl] for sl in sls]
        ks = [k_s[rows, sl] for sl in sls]
        vs = [v_s[rows, sl] for sl in sls]
        b_cols = [beta[:, h:h + 1] for h in heads]
        g_cols = [gam[:, GDN_HEADS + h:GDN_HEADS + h + 1] for h in heads]
        egs = [egam[:, GDN_HEADS + h:GDN_HEADS + h + 1] for h in heads]
        dec_incl = [jnp.exp(jnp.where(incl, g_cols[h] - gam_t[GDN_HEADS + h:GDN_HEADS + h + 1, :], -jnp.inf))
                    for h in heads]
        kk = [_mm_nt(ks[h], ks[h]) for h in heads]
        qk = [_mm_nt(qs[h], ks[h]) * dec_incl[h] for h in heads]
        a = [b_cols[h] * jnp.where(strict, dec_incl[h], 0.0) * kk[h] for h in heads]
        tinv = _unit_lower_inverses(a, lc)
        rhs = [jnp.concatenate([b_cols[h] * vs[h], (b_cols[h] * egs[h]) * ks[h]], axis=-1) for h in heads]
        x = [_mm(tinv[h], rhs[h]) for h in heads]
        s = [st_s[h] for h in heads]
        w = [x[h][:, :GDN_DV] - _mm(x[h][:, GDN_DV:], s[h]) for h in heads]
        o = [_mm(qs[h] * egs[h], s[h]) + _mm(qk[h], w[h]) for h in heads]
        for h in heads:
            gl = GDN_HEADS + h
            k_end = ks[h] * eend[:, gl:gl + 1]
            st_s[h] = cdec[:, gl:gl + 1] * s[h] + _mm_tn(k_end, w[h])
        for h in heads:
            on = o[h] * lax.rsqrt(jnp.mean(o[h] * o[h], axis=-1, keepdims=True) + EPS) * ng_ref[...]
            o_ref[0, rows, sls[h]] = (on * _silu(z_ref[0, rows, sls[h]])).astype(o_ref.dtype)
        return carry

    lax.fori_loop(0, tt // lc, chunk, 0)

    @pl.when(t == pl.num_programs(1) - 1)
    def _():
        sout_ref[0] = st_s[...]


def _gdn_core(proj, conv_hist, s0, conv_w, a_log, dt_bias, norm_g):
    b, t, _ = proj.shape
    lc = min(CHUNK, t)
    tt = min(ROW_TILE, t)
    nq = GDN_QK // GDN_QK
    lane_pad = jnp.zeros((LANES - 2 * GDN_HEADS,), F32)
    alog_row = jnp.concatenate([jnp.zeros((GDN_HEADS,), F32), a_log, lane_pad]).reshape(1, LANES)
    dtb_row = jnp.concatenate([jnp.zeros((GDN_HEADS,), F32), dt_bias, lane_pad]).reshape(1, LANES)
    gate_blk = (GDN_CONV_DIM + GDN_VD) // LANES
    col = lambda j: pl.BlockSpec((1, tt, GDN_QK), lambda i, s: (i, s, j))
    hist = lambda j: pl.BlockSpec((1, CONV_W - 1, GDN_QK), lambda i, s: (i, 0, j))
    cw = lambda j: pl.BlockSpec((CONV_W, GDN_QK), lambda i, s: (0, j))
    row = pl.BlockSpec((1, LANES), lambda i, s: (0, 0))
    state = pl.BlockSpec((1, GDN_HEADS, GDN_DK, GDN_DV), lambda i, s: (i, 0, 0, 0))
    del nq
    return pl.pallas_call(
        functools.partial(_gdn_kernel, lc),
        grid=(b, t // tt),
        in_specs=[col(0), col(1), col(2), col(3),
                  pl.BlockSpec((1, tt, LANES), lambda i, s: (i, s, gate_blk)),
                  hist(0), hist(1), hist(2), cw(0), cw(1), cw(2), state, row, row, row],
        out_specs=[pl.BlockSpec((1, tt, GDN_VD), lambda i, s: (i, s, 0)), state],
        out_shape=[jax.ShapeDtypeStruct((b, t, GDN_VD), BF16),
                   jax.ShapeDtypeStruct((b, GDN_HEADS, GDN_DK, GDN_DV), F32)],
        scratch_shapes=[pltpu.VMEM((CONV_PAD + tt, GDN_QK), F32)] * 3
        + [pltpu.VMEM((tt, GDN_QK), F32)] * 3
        + [pltpu.VMEM((tt, LANES), F32)] * 2
        + [pltpu.VMEM((GDN_HEADS, GDN_DK, GDN_DV), F32)],
        compiler_params=_cparams(("arbitrary", "arbitrary")),
        name="gdn_core",
    )(proj, proj, proj, proj, proj, conv_hist, conv_hist, conv_hist, conv_w, conv_w, conv_w,
      s0, alog_row, dtb_row, norm_g.reshape(1, GDN_DV))


def _ssd_kernel(lc, z_ref, xp_ref, bp_ref, cp_ref, dt_ref, hx_ref, hb_ref, hc_ref,
                wx_ref, wb_ref, wc_ref, bx_ref, bb_ref, bc_ref, h0_ref,
                alog_ref, dtb_ref, dskip_ref, ng_ref,
                y_ref, hout_ref,
                xx_s, xb_s, xc_s, h_s):
    t = pl.program_id(2)
    tt = xp_ref.shape[1]
    first = t == 0
    p = SSD_HEADDIM

    @pl.when(first)
    def _():
        h_s[...] = h0_ref[0].reshape(SSD_HPG * p, SSD_DSTATE)

    x = _silu(_conv_tile(xx_s, xp_ref[0], hx_ref, wx_ref, first, bx_ref[...]))
    bm = _silu(_conv_tile(xb_s, bp_ref[0], hb_ref, wb_ref, first, bb_ref[...]))
    cm = _silu(_conv_tile(xc_s, cp_ref[0], hc_ref, wc_ref, first, bc_ref[...]))
    dt = _softplus(dt_ref[0] + dtb_ref[0])
    da = dt * (-jnp.exp(alog_ref[0]))

    shift = lc.bit_length() - 1
    r = lax.broadcasted_iota(I32, (tt, tt), 0)
    c = lax.broadcasted_iota(I32, (tt, tt), 1)
    same = lax.shift_right_logical(r, shift) == lax.shift_right_logical(c, shift)
    incl = same & (r >= c)
    gam = _mm_f32(jnp.where(incl, 1.0, 0.0), da)
    gend = _mm_f32(jnp.where(same, 1.0, 0.0), da)
    gam_t = gam.T
    low_half = lax.broadcasted_iota(I32, (tt, LANES), 1) < p

    def per_column(m):
        cols = [jnp.broadcast_to(m[:, e:e + 1], (tt, LANES)) for e in range(SSD_HPG)]
        return jnp.concatenate([jnp.where(low_half, cols[2 * i], cols[2 * i + 1])
                                for i in range(SSD_HPG // 2)], axis=-1)

    eg_x = per_column(jnp.exp(gam))
    xdt = x * per_column(dt)
    xe = xdt * per_column(jnp.exp(gend - gam))
    cb = _mm_nt(cm, bm)
    parts = []
    for pr in range(SSD_HPG // 2):
        cols = slice(pr * LANES, (pr + 1) * LANES)
        lms = [jnp.exp(jnp.where(incl, gam[:, e:e + 1] - gam_t[e:e + 1, :], -jnp.inf)) for e in (2 * pr, 2 * pr + 1)]
        ys = [_mm(cb * lm, xdt[:, cols]) for lm in lms]
        parts.append(jnp.where(low_half, ys[0], ys[1]))
    y_intra = jnp.concatenate(parts, axis=-1)

    h = h_s[...]
    inter = []
    for ci in range(tt // lc):
        rs = slice(ci * lc, (ci + 1) * lc)
        inter.append(_mm_nt(cm[rs], h))
        st = _mm_tn(xe[rs], bm[rs])
        cdec = jnp.exp(gend[ci * lc:ci * lc + 1, :])
        h = jnp.concatenate([h[e * p:(e + 1) * p] * cdec[:, e:e + 1] + st[e * p:(e + 1) * p]
                             for e in range(SSD_HPG)], axis=0)
    h_s[...] = h
    y = y_intra + jnp.concatenate(inter, axis=0) * eg_x + x * dskip_ref[0]
    y = y * _silu(z_ref[0])
    y = y * lax.rsqrt(jnp.mean(y * y, axis=-1, keepdims=True) + EPS) * ng_ref[0]
    y_ref[0] = y.astype(y_ref.dtype)

    @pl.when(t == pl.num_programs(2) - 1)
    def _():
        hout_ref[0] = h_s[...].reshape(SSD_HPG, p, SSD_DSTATE)


def _ssd_core(proj, conv_hist, h0, conv_w, conv_b, dt_bias, a_log, d_skip, norm_g):
    b, t, _ = proj.shape
    lc = min(CHUNK, t)
    tt = min(ROW_TILE, t)
    g_n = SSD_GROUPS
    gs = SSD_GS
    xblk = SSD_INNER // gs
    bblk = 2 * SSD_INNER // LANES
    cblk = bblk + g_n
    dblk = cblk + g_n

    def per_group(v):
        return jnp.pad(v.reshape(g_n, 1, SSD_HPG), ((0, 0), (0, 0), (0, LANES - SSD_HPG)))

    dskip_x = jnp.repeat(d_skip, SSD_HEADDIM).reshape(g_n, 1, gs)
    grow = pl.BlockSpec((1, 1, LANES), lambda i, g, s: (g, 0, 0))
    cwb = conv_b.reshape(1, SSD_CONV_DIM)
    in_specs = [
        pl.BlockSpec((1, tt, gs), lambda i, g, s: (i, s, g)),
        pl.BlockSpec((1, tt, gs), lambda i, g, s: (i, s, xblk + g)),
        pl.BlockSpec((1, tt, LANES), lambda i, g, s: (i, s, bblk + g)),
        pl.BlockSpec((1, tt, LANES), lambda i, g, s: (i, s, cblk + g)),
        pl.BlockSpec((1, tt, LANES), lambda i, g, s: (i, s, dblk + g)),
        pl.BlockSpec((1, CONV_W - 1, gs), lambda i, g, s: (i, 0, g)),
        pl.BlockSpec((1, CONV_W - 1, LANES), lambda i, g, s: (i, 0, SSD_INNER // LANES + g)),
        pl.BlockSpec((1, CONV_W - 1, LANES), lambda i, g, s: (i, 0, SSD_INNER // LANES + g_n + g)),
        pl.BlockSpec((CONV_W, gs), lambda i, g, s: (0, g)),
        pl.BlockSpec((CONV_W, LANES), lambda i, g, s: (0, SSD_INNER // LANES + g)),
        pl.BlockSpec((CONV_W, LANES), lambda i, g, s: (0, SSD_INNER // LANES + g_n + g)),
        pl.BlockSpec((1, gs), lambda i, g, s: (0, g)),
        pl.BlockSpec((1, LANES), lambda i, g, s: (0, SSD_INNER // LANES + g)),
        pl.BlockSpec((1, LANES), lambda i, g, s: (0, SSD_INNER // LANES + g_n + g)),
        pl.BlockSpec((1, SSD_HPG, SSD_HEADDIM, SSD_DSTATE), lambda i, g, s: (i, g, 0, 0)),
        grow, grow,
        pl.BlockSpec((1, 1, gs), lambda i, g, s: (g, 0, 0)),
        pl.BlockSpec((1, 1, gs), lambda i, g, s: (g, 0, 0)),
    ]
    return pl.pallas_call(
        functools.partial(_ssd_kernel, lc),
        grid=(b, g_n, t // tt),
        in_specs=in_specs,
        out_specs=[pl.BlockSpec((1, tt, gs), lambda i, g, s: (i, s, g)),
                   pl.BlockSpec((1, SSD_HPG, SSD_HEADDIM, SSD_DSTATE), lambda i, g, s: (i, g, 0, 0))],
        out_shape=[jax.ShapeDtypeStruct((b, t, SSD_INNER), BF16),
                   jax.ShapeDtypeStruct((b, SSD_HEADS, SSD_HEADDIM, SSD_DSTATE), F32)],
        scratch_shapes=[pltpu.VMEM((CONV_PAD + tt, gs), F32),
                        pltpu.VMEM((CONV_PAD + tt, LANES), F32),
                        pltpu.VMEM((CONV_PAD + tt, LANES), F32),
                        pltpu.VMEM((SSD_HPG * SSD_HEADDIM, SSD_DSTATE), F32)],
        compiler_params=_cparams(("arbitrary", "arbitrary", "arbitrary")),
        name="ssd_core",
    )(proj, proj, proj, proj, proj, conv_hist, conv_hist, conv_hist, conv_w, conv_w, conv_w,
      cwb, cwb, cwb, h0, per_group(a_log), per_group(dt_bias), dskip_x,
      norm_g.reshape(g_n, 1, gs))


def _outproj_kernel(o_ref, x_ref, g1_ref, w_ref, n2_ref, sc_ref, sh_ref, wr_ref, br_ref,
                    x1_ref, h2_ref, ri_ref, rw_ref, cnt_ref, cnt_s):
    bb, tt, d = x_ref.shape
    tm = bb * tt
    step = pl.program_id(0) * pl.num_programs(1) + pl.program_id(1)

    @pl.when(step == 0)
    def _():
        cnt_s[...] = jnp.zeros_like(cnt_s)

    out = jnp.dot(o_ref[...].reshape(tm, o_ref.shape[-1]), w_ref[...], preferred_element_type=F32)
    x1 = x_ref[...] + g1_ref[...] * out.reshape(bb, tt, d)
    x1_ref[...] = x1
    y = x1 * lax.rsqrt(jnp.mean(x1 * x1, axis=-1, keepdims=True) + EPS) * n2_ref[...]
    h2 = (y * (1.0 + sc_ref[...]) + sh_ref[...]).reshape(tm, d)
    h2_ref[...] = h2

    logits = _mm(h2, wr_ref[...]) + br_ref[...]
    lane = lax.broadcasted_iota(I32, (tm, LANES), 1)
    lane_f = lane.astype(F32)
    neg = -jnp.inf
    gl = jnp.where(lane < MOE_GROUPS, logits, neg)
    ge = jnp.exp(gl - jnp.max(gl, axis=-1, keepdims=True))
    grp_p = ge / jnp.sum(ge, axis=-1, keepdims=True)
    gp = jnp.max(grp_p, axis=-1, keepdims=True)
    gi = jnp.min(jnp.where(grp_p == gp, lane_f, float(LANES)), axis=-1, keepdims=True).astype(I32)
    lo = MOE_GROUPS + gi * MOE_PER_GROUP
    emask = (lane >= lo) & (lane < lo + MOE_PER_GROUP)
    sel = jnp.where(emask, logits, neg)
    se = jnp.exp(sel - jnp.max(sel, axis=-1, keepdims=True))
    p = jnp.where(emask, se / jnp.sum(se, axis=-1, keepdims=True), -1.0)
    v1 = jnp.max(p, axis=-1, keepdims=True)
    i1 = jnp.min(jnp.where(p == v1, lane_f, float(LANES)), axis=-1, keepdims=True).astype(I32)
    p2 = jnp.where(lane == i1, -1.0, p)
    v2 = jnp.max(p2, axis=-1, keepdims=True)
    i2 = jnp.min(jnp.where(p2 == v2, lane_f, float(LANES)), axis=-1, keepdims=True).astype(I32)
    den = v1 + v2
    w1 = gp * v1 / den
    w2 = gp * v2 / den
    e1 = i1 - MOE_GROUPS
    e2 = i2 - MOE_GROUPS

    hit1 = lane == e1
    hit2 = lane == e2
    onehot = jnp.where(hit1 | hit2, 1.0, 0.0)
    r = lax.broadcasted_iota(I32, (tm, tm), 0)
    c = lax.broadcasted_iota(I32, (tm, tm), 1)
    before = _mm(jnp.where(r > c, 1.0, 0.0), onehot) + cnt_s[...]
    r1 = jnp.sum(jnp.where(hit1, before, 0.0), axis=-1, keepdims=True).astype(I32)
    r2 = jnp.sum(jnp.where(hit2, before, 0.0), axis=-1, keepdims=True).astype(I32)
    cnt_s[...] = cnt_s[...] + jnp.sum(onehot, axis=0, keepdims=True)
    cnt_ref[...] = cnt_s[...].astype(I32)
    ri_ref[...] = jnp.where(lane == 0, e1, jnp.where(lane == 1, e2, jnp.where(lane == 2, r1, jnp.where(lane == 3, r2, 0))))
    rw_ref[...] = jnp.where(lane == 0, w1, jnp.where(lane == 1, w2, 0.0))


def _out_proj_route(o, x, mod, w_bf16, norm2_g, w_router, b_router):
    b, t, d = x.shape
    kdim = o.shape[-1]
    bb, tt = _tiles(b, t)
    tm = bb * tt
    nt = t // tt
    tok = lambda i, j: (i * nt + j, 0)
    modspec = lambda idx: pl.BlockSpec((bb, 1, d), lambda i, j: (i, 0, idx))
    return pl.pallas_call(
        _outproj_kernel,
        grid=(b // bb, nt),
        in_specs=[pl.BlockSpec((bb, tt, kdim), lambda i, j: (i, j, 0)),
                  pl.BlockSpec((bb, tt, d), lambda i, j: (i, j, 0)),
                  modspec(2),
                  pl.BlockSpec((kdim, d), lambda i, j: (0, 0)),
                  pl.BlockSpec((1, d), lambda i, j: (0, 0)),
                  modspec(4), modspec(3),
                  pl.BlockSpec((d, LANES), lambda i, j: (0, 0)),
                  pl.BlockSpec((1, LANES), lambda i, j: (0, 0))],
        out_specs=[pl.BlockSpec((bb, tt, d), lambda i, j: (i, j, 0)),
                   pl.BlockSpec((tm, d), tok),
                   pl.BlockSpec((tm, LANES), tok),
                   pl.BlockSpec((tm, LANES), tok),
                   pl.BlockSpec((1, LANES), lambda i, j: (0, 0))],
        out_shape=[jax.ShapeDtypeStruct((b, t, d), F32),
                   jax.ShapeDtypeStruct((b * t, d), F32),
                   jax.ShapeDtypeStruct((b * t, LANES), I32),
                   jax.ShapeDtypeStruct((b * t, LANES), F32),
                   jax.ShapeDtypeStruct((1, LANES), I32)],
        scratch_shapes=[pltpu.VMEM((1, LANES), F32)],
        compiler_params=_cparams(("arbitrary", "arbitrary")),
        name="out_proj_route",
    )(o, x, mod, w_bf16, norm2_g.reshape(1, d), mod, mod, w_router, b_router)


def _dispatch_kernel(dest_ref, h2_ref, xb_in_ref, xb_ref, sem):
    del xb_in_ref
    tm = h2_ref.shape[0]

    def copy(r, k):
        return pltpu.make_async_copy(h2_ref.at[pl.ds(r, 1), :],
                                     xb_ref.at[pl.ds(dest_ref[0, 0, MOE_TOPK * r + k], 1), :], sem)

    def start(r, carry):
        for k in range(MOE_TOPK):
            copy(r, k).start()
        return carry

    def wait(r, carry):
        for k in range(MOE_TOPK):
            copy(r, k).wait()
        return carry

    lax.fori_loop(0, tm, start, 0)
    lax.fori_loop(0, tm, wait, 0)


def _dispatch(h2, dest_tiles, cap):
    n, d = h2.shape
    nt, _, per = dest_tiles.shape
    tm = per // MOE_TOPK
    return pl.pallas_call(
        _dispatch_kernel,
        grid=(nt,),
        in_specs=[pl.BlockSpec((1, 1, per), lambda i: (i, 0, 0), memory_space=pltpu.SMEM),
                  pl.BlockSpec((tm, d), lambda i: (i, 0)),
                  pl.BlockSpec(memory_space=pl.ANY)],
        out_specs=pl.BlockSpec(memory_space=pl.ANY),
        out_shape=jax.ShapeDtypeStruct((cap, d), F32),
        scratch_shapes=[pltpu.SemaphoreType.DMA],
        input_output_aliases={2: 0},
        compiler_params=_cparams(("arbitrary",)),
        name="moe_dispatch",
    )(dest_tiles, h2, jnp.zeros((cap, d), F32))


def _expert_kernel(be_ref, nu_ref, x_ref, wg_ref, wu_ref, wd_ref, y_ref):
    used = pl.program_id(0) < nu_ref[0]

    @pl.when(used)
    def _():
        x = x_ref[...]
        hid = _silu(_mm(x, wg_ref[0])) * _mm(x, wu_ref[0])
        y_ref[...] = _mm(hid, wd_ref[0])

    @pl.when(jnp.logical_not(used))
    def _():
        y_ref[...] = jnp.zeros_like(y_ref)


def _experts(xb, block_e, n_used, w_gate, w_up, w_down):
    cap, d = xb.shape
    nb = cap // EXPERT_ROWS
    blk = lambda i, be, nu: (jnp.minimum(i, nu[0] - 1), 0)
    wsel = lambda i, be, nu: (be[jnp.minimum(i, nu[0] - 1)], 0, 0)
    return pl.pallas_call(
        _expert_kernel,
        grid_spec=pltpu.PrefetchScalarGridSpec(
            num_scalar_prefetch=2,
            grid=(nb,),
            in_specs=[pl.BlockSpec((EXPERT_ROWS, d), blk),
                      pl.BlockSpec((1, d, D_EXPERT), wsel),
                      pl.BlockSpec((1, d, D_EXPERT), wsel),
                      pl.BlockSpec((1, D_EXPERT, d), wsel)],
            out_specs=pl.BlockSpec((EXPERT_ROWS, d), lambda i, be, nu: (i, 0))),
        out_shape=jax.ShapeDtypeStruct((cap, d), F32),
        compiler_params=_cparams(("arbitrary",)),
        name="moe_experts",
    )(block_e, n_used, xb, w_gate, w_up, w_down)


def _combine_kernel(final, dest_ref, x1_ref, rw_ref, g2_ref, fg_ref, yb_ref, o_ref, buf_s, sem):
    bb, tt, d = x1_ref.shape
    tm = bb * tt

    def copy(r, k):
        return pltpu.make_async_copy(yb_ref.at[pl.ds(dest_ref[0, 0, MOE_TOPK * r + k], 1), :],
                                     buf_s.at[k, pl.ds(r, 1), :], sem)

    def start(r, carry):
        for k in range(MOE_TOPK):
            copy(r, k).start()
        return carry

    def wait(r, carry):
        for k in range(MOE_TOPK):
            copy(r, k).wait()
        return carry

    lax.fori_loop(0, tm, start, 0)
    lax.fori_loop(0, tm, wait, 0)
    rw = rw_ref[...]
    moe = buf_s[0] * rw[:, 0:1] + buf_s[1] * rw[:, 1:2]
    x2 = x1_ref[...] + g2_ref[...] * moe.reshape(bb, tt, d)
    if final:
        x2 = x2 * lax.rsqrt(jnp.mean(x2 * x2, axis=-1, keepdims=True) + EPS) * fg_ref[...]
    o_ref[...] = x2


def _combine(x1, route_w, mod, yb, dest_tiles, final_g, final):
    b, t, d = x1.shape
    bb, tt = _tiles(b, t)
    tm = bb * tt
    nt = t // tt
    per = dest_tiles.shape[-1]
    return pl.pallas_call(
        functools.partial(_combine_kernel, final),
        grid=(b // bb, nt),
        in_specs=[pl.BlockSpec((1, 1, per), lambda i, j: (i * nt + j, 0, 0), memory_space=pltpu.SMEM),
                  pl.BlockSpec((bb, tt, d), lambda i, j: (i, j, 0)),
                  pl.BlockSpec((tm, LANES), lambda i, j: (i * nt + j, 0)),
                  pl.BlockSpec((bb, 1, d), lambda i, j: (i, 0, 5)),
                  pl.BlockSpec((1, d), lambda i, j: (0, 0)),
                  pl.BlockSpec(memory_space=pl.ANY)],
        out_specs=pl.BlockSpec((bb, tt, d), lambda i, j: (i, j, 0)),
        out_shape=jax.ShapeDtypeStruct((b, t, d), F32),
        scratch_shapes=[pltpu.VMEM((MOE_TOPK, tm, d), F32), pltpu.SemaphoreType.DMA],
        compiler_params=_cparams(("arbitrary", "arbitrary")),
        name="moe_combine",
    )(dest_tiles, x1, route_w, mod, final_g.reshape(1, d), yb)


def _moe(x1, h2, route_i, route_w, counts, mod, w_gate, w_up, w_down, final_g, final):
    b, t, d = x1.shape
    n = b * t
    bb, tt = _tiles(b, t)
    tm = bb * tt
    n_asg = n * MOE_TOPK
    nb = (n_asg + N_EXPERTS * (EXPERT_ROWS - 1) + EXPERT_ROWS - 1) // EXPERT_ROWS
    cap = nb * EXPERT_ROWS
    cnt = counts[0, :N_EXPERTS]
    padded = (cnt + EXPERT_ROWS - 1) // EXPERT_ROWS * EXPERT_ROWS
    ends = jnp.cumsum(padded)
    pstart = ends - padded
    eid = route_i[:, :MOE_TOPK]
    rank = route_i[:, MOE_TOPK:2 * MOE_TOPK]
    first_slot = jnp.sum(jnp.where(eid[..., None] == jnp.arange(N_EXPERTS, dtype=I32), pstart, 0), axis=-1)
    dest_tiles = (first_slot + rank).astype(I32).reshape(n // tm, 1, tm * MOE_TOPK)
    blk_start = jnp.arange(nb, dtype=I32) * EXPERT_ROWS
    block_e = jnp.minimum(jnp.sum(blk_start[:, None] >= ends[None, :], axis=-1), N_EXPERTS - 1).astype(I32)
    n_used = (ends[-1:] // EXPERT_ROWS).astype(I32)
    xb = _dispatch(h2, dest_tiles, cap)
    yb = _experts(xb, block_e, n_used, w_gate, w_up, w_down)
    return _combine(x1, route_w, mod, yb, dest_tiles, final_g, final)


def _pad_cols(w, n):
    return jnp.pad(w, ((0, 0), (0, n - w.shape[1])))


def _trunk(x, mods, gdn_conv, gdn_ssm, ssd_conv, ssd_ssm, w):
    b, t, d = x.shape
    gdn_main = GDN_CONV_DIM + GDN_VD
    w_in = jnp.concatenate([w['gdn_w_in'][0][:, :gdn_main], _pad_cols(w['gdn_w_in'][0][:, gdn_main:], LANES)],
                           axis=1).astype(BF16)
    proj = _in_proj(x, w['norm1_g'][0], mods[0], 1, 0, w_in)
    o, gdn_state = _gdn_core(proj, gdn_conv[0], gdn_ssm[0], w['gdn_conv_w'][0], w['gdn_A_log'][0],
                             w['gdn_dt_bias'][0], w['gdn_norm_g'][0])
    gdn_hist = proj[:, t - (CONV_W - 1):, :GDN_CONV_DIM]
    x = _layer_tail(0, o, x, mods[0], w['gdn_w_out'][0], w, False)
    ssd_main = SSD_INNER + SSD_CONV_DIM
    dt_cols = [_pad_cols(w['ssd_w_in'][0][:, ssd_main + g * SSD_HPG: ssd_main + (g + 1) * SSD_HPG], LANES)
               for g in range(SSD_GROUPS)]
    w_in = jnp.concatenate([w['ssd_w_in'][0][:, :ssd_main]] + dt_cols, axis=1).astype(BF16)
    proj = _in_proj(x, w['norm1_g'][1], mods[1], 1, 0, w_in)
    y, ssd_state = _ssd_core(proj, ssd_conv[0], ssd_ssm[0], w['ssd_conv_w'][0], w['ssd_conv_b'][0],
                             w['ssd_dt_bias'][0], w['ssd_A_log'][0], w['ssd_D'][0], w['ssd_norm_g'][0])
    ssd_hist = proj[:, t - (CONV_W - 1):, SSD_INNER:ssd_main]
    y_out = _layer_tail(1, y, x, mods[1], w['ssd_w_out'][0], w, True)
    return y_out, gdn_hist[None], gdn_state[None], ssd_hist[None], ssd_state[None]


def _layer_tail(i, mixed, x, mod, w_out, w, final):
    d = x.shape[-1]
    w_router = _pad_cols(jnp.concatenate([w['moe_w_group'][i], w['moe_w_expert'][i]], axis=1), LANES)
    b_router = _pad_cols(jnp.concatenate([w['moe_b_group'][i], w['moe_b_expert'][i]]).reshape(1, -1), LANES)
    x1, h2, route_i, route_w, counts = _out_proj_route(mixed, x, mod, w_out.astype(BF16), w['norm2_g'][i],
                                                       w_router, b_router)
    del d
    return _moe(x1, h2, route_i, route_w, counts, mod, w['moe_w_gate'][i], w['moe_w_up'][i], w['moe_w_down'][i],
                w['final_norm_g'], final)


def kernel(x_prompt, x_sample, state_gdn_conv, state_gdn_ssm, state_ssd_conv, state_ssd_ssm, c_prompt, c_sample,
           norm1_g, norm2_g, ada_w, ada_b, gdn_w_in, gdn_conv_w, gdn_A_log, gdn_dt_bias, gdn_norm_g, gdn_w_out,
           ssd_w_in, ssd_conv_w, ssd_conv_b, ssd_dt_bias, ssd_A_log, ssd_D, ssd_norm_g, ssd_w_out,
           moe_w_group, moe_b_group, moe_w_expert, moe_b_expert, moe_w_gate, moe_w_up, moe_w_down, final_norm_g):
    w = {'norm1_g': norm1_g, 'norm2_g': norm2_g, 'gdn_w_in': gdn_w_in, 'gdn_conv_w': gdn_conv_w,
         'gdn_A_log': gdn_A_log, 'gdn_dt_bias': gdn_dt_bias, 'gdn_norm_g': gdn_norm_g, 'gdn_w_out': gdn_w_out,
         'ssd_w_in': ssd_w_in, 'ssd_conv_w': ssd_conv_w, 'ssd_conv_b': ssd_conv_b, 'ssd_dt_bias': ssd_dt_bias,
         'ssd_A_log': ssd_A_log, 'ssd_D': ssd_D, 'ssd_norm_g': ssd_norm_g, 'ssd_w_out': ssd_w_out,
         'moe_w_group': moe_w_group, 'moe_b_group': moe_b_group, 'moe_w_expert': moe_w_expert,
         'moe_b_expert': moe_b_expert, 'moe_w_gate': moe_w_gate, 'moe_w_up': moe_w_up, 'moe_w_down': moe_w_down,
         'final_norm_g': final_norm_g}
    bp = x_prompt.shape[0]
    bs = x_sample.shape[0]
    dt_ = x_prompt.dtype
    n_seq = bp + bs
    n_pad = -n_seq % SUBLANES
    c_all = jnp.concatenate([c_prompt, c_sample, jnp.zeros((n_pad, c_prompt.shape[1]), dt_)], axis=0)
    mod_all = _ada_mod(c_all, ada_w, ada_b)
    mods_p = [mod_all[l, :bp][:, None, :] for l in range(DEPTH)]
    mods_s = [mod_all[l, bp:n_seq][:, None, :] for l in range(DEPTH)]
    n_gdn = state_gdn_conv.shape[0]
    n_ssd = state_ssd_conv.shape[0]
    z_gc = jnp.zeros((n_gdn, bp) + state_gdn_conv.shape[2:], dt_)
    z_gs = jnp.zeros((n_gdn, bp) + state_gdn_ssm.shape[2:], dt_)
    z_sc = jnp.zeros((n_ssd, bp) + state_ssd_conv.shape[2:], dt_)
    z_ss = jnp.zeros((n_ssd, bp) + state_ssd_ssm.shape[2:], dt_)
    y_p, p_gc, p_gs, p_sc, p_ss = _trunk(x_prompt, mods_p, z_gc, z_gs, z_sc, z_ss, w)
    y_s, s_gc, s_gs, s_sc, s_ss = _trunk(x_sample, mods_s, state_gdn_conv, state_gdn_ssm,
                                         state_ssd_conv, state_ssd_ssm, w)
    return (y_p, y_s, p_gc, p_gs, p_sc, p_ss, s_gc, s_gs, s_sc, s_ss)
```

```python
import functools

import jax
import jax.numpy as jnp
from jax import lax
from jax.experimental import pallas as pl
from jax.experimental.pallas import tpu as pltpu

F32 = jnp.float32
BF16 = jnp.bfloat16
I32 = jnp.int32

D_MODEL = 1024
DEPTH = 2
CHUNK = 64
CONV_W = 4
EPS = 1e-6
GDN_HEADS = 8
GDN_DK = 128
GDN_DV = 128
GDN_QK = GDN_HEADS * GDN_DK
GDN_VD = GDN_HEADS * GDN_DV
GDN_CONV_DIM = 2 * GDN_QK + GDN_VD
SSD_INNER = 2 * D_MODEL
SSD_HEADDIM = 64
SSD_HEADS = SSD_INNER // SSD_HEADDIM
SSD_GROUPS = 4
SSD_HPG = SSD_HEADS // SSD_GROUPS
SSD_DSTATE = 128
SSD_GS = SSD_INNER // SSD_GROUPS
SSD_CONV_DIM = SSD_INNER + 2 * SSD_GROUPS * SSD_DSTATE
MOE_GROUPS = 4
MOE_PER_GROUP = 8
N_EXPERTS = MOE_GROUPS * MOE_PER_GROUP
MOE_TOPK = 2
D_EXPERT = 512

LANES = 128
SUBLANES = 8
VMEM_LIMIT = 48 * 1024 * 1024

ROW_TILE = 256
EXPERT_ROWS = 256
CONV_PAD = SUBLANES


def _cparams(sem):
    return pltpu.CompilerParams(dimension_semantics=sem, vmem_limit_bytes=VMEM_LIMIT)


def _mm(a, b):
    return jnp.dot(a.astype(BF16), b.astype(BF16), preferred_element_type=F32)


def _mm_nt(a, b):
    return lax.dot_general(a.astype(BF16), b.astype(BF16), (((1,), (1,)), ((), ())),
                           preferred_element_type=F32)


def _mm_tn(a, b):
    return lax.dot_general(a.astype(BF16), b.astype(BF16), (((0,), (0,)), ((), ())),
                           preferred_element_type=F32)


def _mm_f32(a, b):
    return jnp.dot(a, b, preferred_element_type=F32, precision=lax.Precision.HIGHEST)


def _silu(x):
    return x * jax.nn.sigmoid(x)


def _softplus(x):
    return jnp.maximum(x, 0.0) + jnp.log1p(jnp.exp(-jnp.abs(x)))


def _tiles(b, t):
    if t >= ROW_TILE:
        assert t % ROW_TILE == 0
        return 1, ROW_TILE
    assert ROW_TILE % t == 0 and b % (ROW_TILE // t) == 0
    return ROW_TILE // t, t


def _ada_kernel(c_ref, w_ref, b_ref, o_ref):
    o_ref[0] = _mm(_silu(c_ref[...]), w_ref[0]) + b_ref[0]


def _ada_mod(c_all, ada_w, ada_b):
    bp, d = c_all.shape
    n = ada_w.shape[-1]
    tn = 1024
    return pl.pallas_call(
        _ada_kernel,
        grid=(DEPTH, n // tn),
        in_specs=[pl.BlockSpec((bp, d), lambda l, j: (0, 0)),
                  pl.BlockSpec((1, d, tn), lambda l, j: (l, 0, j)),
                  pl.BlockSpec((1, 1, tn), lambda l, j: (l, 0, j))],
        out_specs=pl.BlockSpec((1, bp, tn), lambda l, j: (l, 0, j)),
        out_shape=jax.ShapeDtypeStruct((DEPTH, bp, n), F32),
        compiler_params=_cparams(("arbitrary", "arbitrary")),
        name="ada_mod",
    )(c_all, ada_w, ada_b.reshape(DEPTH, 1, n))


def _inproj_kernel(x_ref, g_ref, sc_ref, sh_ref, w_ref, o_ref):
    bb, tt, d = x_ref.shape
    x = x_ref[...]
    y = x * lax.rsqrt(jnp.mean(x * x, axis=-1, keepdims=True) + EPS) * g_ref[...]
    h = y * (1.0 + sc_ref[...]) + sh_ref[...]
    o = _mm(h.reshape(bb * tt, d), w_ref[...])
    o_ref[...] = o.reshape(bb, tt, o.shape[-1])


def _in_proj(x, norm_g, mod, sc_idx, sh_idx, w_bf16):
    b, t, d = x.shape
    n = w_bf16.shape[1]
    bb, tt = _tiles(b, t)
    return pl.pallas_call(
        _inproj_kernel,
        grid=(b // bb, t // tt),
        in_specs=[pl.BlockSpec((bb, tt, d), lambda i, j: (i, j, 0)),
                  pl.BlockSpec((1, d), lambda i, j: (0, 0)),
                  pl.BlockSpec((bb, 1, d), lambda i, j: (i, 0, sc_idx)),
                  pl.BlockSpec((bb, 1, d), lambda i, j: (i, 0, sh_idx)),
                  pl.BlockSpec((d, n), lambda i, j: (0, 0))],
        out_specs=pl.BlockSpec((bb, tt, n), lambda i, j: (i, j, 0)),
        out_shape=jax.ShapeDtypeStruct((b, t, n), F32),
        compiler_params=_cparams(("arbitrary", "arbitrary")),
        name="in_proj",
    )(x, norm_g.reshape(1, d), mod, mod, w_bf16)


def _conv_tile(xp_ref, x_tile, hist_ref, w_ref, first, bias=None):
    tt = x_tile.shape[0]
    h0 = CONV_PAD - (CONV_W - 1)

    @pl.when(first)
    def _():
        xp_ref[h0:CONV_PAD, :] = hist_ref[0]

    xp_ref[CONV_PAD:CONV_PAD + tt, :] = x_tile
    y = xp_ref[h0:h0 + tt, :] * w_ref[0:1, :]
    for i in range(1, CONV_W):
        y = y + xp_ref[h0 + i:h0 + i + tt, :] * w_ref[i:i + 1, :]
    xp_ref[h0:CONV_PAD, :] = xp_ref[tt + h0:tt + CONV_PAD, :]
    if bias is not None:
        y = y + bias
    return y


def _tri_masks(lc):
    r = lax.broadcasted_iota(I32, (lc, lc), 0)
    c = lax.broadcasted_iota(I32, (lc, lc), 1)
    return r >= c, r > c


def _split_bf16(a):
    hi = a.astype(BF16)
    return hi, (a - hi.astype(F32)).astype(BF16)


def _mm_split(a, b):
    ah, al = _split_bf16(a)
    bh, bl = _split_bf16(b)
    dot = functools.partial(jnp.dot, preferred_element_type=F32)
    return dot(ah, bh) + dot(ah, bl) + dot(al, bh)


def _unit_lower_inverses(mats, lc):
    r = lax.broadcasted_iota(I32, (lc, lc), 0)
    c = lax.broadcasted_iota(I32, (lc, lc), 1)
    eye = jnp.where(r == c, 1.0, 0.0)
    invs = [eye - a for a in mats]
    pows = list(mats)
    n = 2
    while n < lc:
        pows = [_mm_split(p, p) for p in pows]
        invs = [inv + _mm_split(inv, p) for inv, p in zip(invs, pows)]
        n *= 2
    return invs


def _gdn_kernel(lc, qp_ref, kp_ref, vp_ref, z_ref, gt_ref, hq_ref, hk_ref, hv_ref,
                wq_ref, wk_ref, wv_ref, s0_ref, alog_ref, dtb_ref, ng_ref,
                o_ref, sout_ref,
                xq_s, xk_s, xv_s, q_s, k_s, v_s, beta_s, g_s, st_s):
    t = pl.program_id(1)
    tt = qp_ref.shape[1]
    first = t == 0

    @pl.when(first)
    def _():
        st_s[...] = s0_ref[0]

    q = _silu(_conv_tile(xq_s, qp_ref[0], hq_ref, wq_ref, first))
    k = _silu(_conv_tile(xk_s, kp_ref[0], hk_ref, wk_ref, first))
    v_s[...] = _silu(_conv_tile(xv_s, vp_ref[0], hv_ref, wv_ref, first))
    for h in range(GDN_HEADS):
        sl = slice(h * GDN_DK, (h + 1) * GDN_DK)
        qh = q[:, sl]
        kh = k[:, sl]
        q_s[:, sl] = qh * lax.rsqrt(jnp.sum(qh * qh, axis=-1, keepdims=True) + EPS) * (GDN_DK ** -0.5)
        k_s[:, sl] = kh * lax.rsqrt(jnp.sum(kh * kh, axis=-1, keepdims=True) + EPS)
    gates = gt_ref[0]
    beta_s[...] = jax.nn.sigmoid(gates)
    g_s[...] = -jnp.exp(alog_ref[...]) * _softplus(gates + dtb_ref[...])

    incl, strict = _tri_masks(lc)
    tri = jnp.where(incl, 1.0, 0.0)

    def chunk(c, carry):
        r0 = pl.multiple_of(c * lc, lc)
        rows = pl.ds(r0, lc)
        beta = beta_s[rows, :]
        gam = _mm_f32(tri, g_s[rows, :])
        gam_t = gam.T
        egam = jnp.exp(gam)
        glast = gam[lc - 1:lc, :]
        eend = jnp.exp(glast - gam)
        cdec = jnp.exp(glast)
        heads = range(GDN_HEADS)
        sls = [slice(h * GDN_DK, (h + 1) * GDN_DK) for h in heads]
        qs = [q_s[rows, sl] for sl in sls]
        ks = [k_s[rows, sl] for sl in sls]
        vs = [v_s[rows, sl] for sl in sls]
        b_cols = [beta[:, h:h + 1] for h in heads]
        g_cols = [gam[:, GDN_HEADS + h:GDN_HEADS + h + 1] for h in heads]
        egs = [egam[:, GDN_HEADS + h:GDN_HEADS + h + 1] for h in heads]
        dec_incl = [jnp.exp(jnp.where(incl, g_cols[h] - gam_t[GDN_HEADS + h:GDN_HEADS + h + 1, :], -jnp.inf))
                    for h in heads]
        kk = [_mm_nt(ks[h], ks[h]) for h in heads]
        qk = [_mm_nt(qs[h], ks[h]) * dec_incl[h] for h in heads]
        a = [b_cols[h] * jnp.where(strict, dec_incl[h], 0.0) * kk[h] for h in heads]
        tinv = _unit_lower_inverses(a, lc)
        rhs = [jnp.concatenate([b_cols[h] * vs[h], (b_cols[h] * egs[h]) * ks[h]], axis=-1) for h in heads]
        x = [_mm(tinv[h], rhs[h]) for h in heads]
        s = [st_s[h] for h in heads]
        w = [x[h][:, :GDN_DV] - _mm(x[h][:, GDN_DV:], s[h]) for h in heads]
        o = [_mm(qs[h] * egs[h], s[h]) + _mm(qk[h], w[h]) for h in heads]
        for h in heads:
            gl = GDN_HEADS + h
            k_end = ks[h] * eend[:, gl:gl + 1]
            st_s[h] = cdec[:, gl:gl + 1] * s[h] + _mm_tn(k_end, w[h])
        for h in heads:
            on = o[h] * lax.rsqrt(jnp.mean(o[h] * o[h], axis=-1, keepdims=True) + EPS) * ng_ref[...]
            o_ref[0, rows, sls[h]] = (on * _silu(z_ref[0, rows, sls[h]])).astype(o_ref.dtype)
        return carry

    lax.fori_loop(0, tt // lc, chunk, 0)

    @pl.when(t == pl.num_programs(1) - 1)
    def _():
        sout_ref[0] = st_s[...]


def _gdn_core(proj, conv_hist, s0, conv_w, a_log, dt_bias, norm_g):
    b, t, _ = proj.shape
    lc = min(CHUNK, t)
    tt = min(ROW_TILE, t)
    nq = GDN_QK // GDN_QK
    lane_pad = jnp.zeros((LANES - 2 * GDN_HEADS,), F32)
    alog_row = jnp.concatenate([jnp.zeros((GDN_HEADS,), F32), a_log, lane_pad]).reshape(1, LANES)
    dtb_row = jnp.concatenate([jnp.zeros((GDN_HEADS,), F32), dt_bias, lane_pad]).reshape(1, LANES)
    gate_blk = (GDN_CONV_DIM + GDN_VD) // LANES
    col = lambda j: pl.BlockSpec((1, tt, GDN_QK), lambda i, s: (i, s, j))
    hist = lambda j: pl.BlockSpec((1, CONV_W - 1, GDN_QK), lambda i, s: (i, 0, j))
    cw = lambda j: pl.BlockSpec((CONV_W, GDN_QK), lambda i, s: (0, j))
    row = pl.BlockSpec((1, LANES), lambda i, s: (0, 0))
    state = pl.BlockSpec((1, GDN_HEADS, GDN_DK, GDN_DV), lambda i, s: (i, 0, 0, 0))
    del nq
    return pl.pallas_call(
        functools.partial(_gdn_kernel, lc),
        grid=(b, t // tt),
        in_specs=[col(0), col(1), col(2), col(3),
                  pl.BlockSpec((1, tt, LANES), lambda i, s: (i, s, gate_blk)),
                  hist(0), hist(1), hist(2), cw(0), cw(1), cw(2), state, row, row, row],
        out_specs=[pl.BlockSpec((1, tt, GDN_VD), lambda i, s: (i, s, 0)), state],
        out_shape=[jax.ShapeDtypeStruct((b, t, GDN_VD), BF16),
                   jax.ShapeDtypeStruct((b, GDN_HEADS, GDN_DK, GDN_DV), F32)],
        scratch_shapes=[pltpu.VMEM((CONV_PAD + tt, GDN_QK), F32)] * 3
        + [pltpu.VMEM((tt, GDN_QK), F32)] * 3
        + [pltpu.VMEM((tt, LANES), F32)] * 2
        + [pltpu.VMEM((GDN_HEADS, GDN_DK, GDN_DV), F32)],
        compiler_params=_cparams(("arbitrary", "arbitrary")),
        name="gdn_core",
    )(proj, proj, proj, proj, proj, conv_hist, conv_hist, conv_hist, conv_w, conv_w, conv_w,
      s0, alog_row, dtb_row, norm_g.reshape(1, GDN_DV))


def _ssd_kernel(lc, z_ref, xp_ref, bp_ref, cp_ref, dt_ref, hx_ref, hb_ref, hc_ref,
                wx_ref, wb_ref, wc_ref, bx_ref, bb_ref, bc_ref, h0_ref,
                alog_ref, dtb_ref, dskip_ref, ng_ref,
                y_ref, hout_ref,
                xx_s, xb_s, xc_s, h_s):
    t = pl.program_id(2)
    tt = xp_ref.shape[1]
    first = t == 0
    p = SSD_HEADDIM

    @pl.when(first)
    def _():
        h_s[...] = h0_ref[0].reshape(SSD_HPG * p, SSD_DSTATE)

    x = _silu(_conv_tile(xx_s, xp_ref[0], hx_ref, wx_ref, first, bx_ref[...]))
    bm = _silu(_conv_tile(xb_s, bp_ref[0], hb_ref, wb_ref, first, bb_ref[...]))
    cm = _silu(_conv_tile(xc_s, cp_ref[0], hc_ref, wc_ref, first, bc_ref[...]))
    dt = _softplus(dt_ref[0] + dtb_ref[0])
    da = dt * (-jnp.exp(alog_ref[0]))

    shift = lc.bit_length() - 1
    r = lax.broadcasted_iota(I32, (tt, tt), 0)
    c = lax.broadcasted_iota(I32, (tt, tt), 1)
    same = lax.shift_right_logical(r, shift) == lax.shift_right_logical(c, shift)
    incl = same & (r >= c)
    gam = _mm_f32(jnp.where(incl, 1.0, 0.0), da)
    gend = _mm_f32(jnp.where(same, 1.0, 0.0), da)
    gam_t = gam.T
    low_half = lax.broadcasted_iota(I32, (tt, LANES), 1) < p

    def per_column(m):
        cols = [jnp.broadcast_to(m[:, e:e + 1], (tt, LANES)) for e in range(SSD_HPG)]
        return jnp.concatenate([jnp.where(low_half, cols[2 * i], cols[2 * i + 1])
                                for i in range(SSD_HPG // 2)], axis=-1)

    eg_x = per_column(jnp.exp(gam))
    xdt = x * per_column(dt)
    xe = xdt * per_column(jnp.exp(gend - gam))
    cb = _mm_nt(cm, bm)
    parts = []
    for pr in range(SSD_HPG // 2):
        cols = slice(pr * LANES, (pr + 1) * LANES)
        lms = [jnp.exp(jnp.where(incl, gam[:, e:e + 1] - gam_t[e:e + 1, :], -jnp.inf)) for e in (2 * pr, 2 * pr + 1)]
        ys = [_mm(cb * lm, xdt[:, cols]) for lm in lms]
        parts.append(jnp.where(low_half, ys[0], ys[1]))
    y_intra = jnp.concatenate(parts, axis=-1)

    h = h_s[...]
    inter = []
    for ci in range(tt // lc):
        rs = slice(ci * lc, (ci + 1) * lc)
        inter.append(_mm_nt(cm[rs], h))
        st = _mm_tn(xe[rs], bm[rs])
        cdec = jnp.exp(gend[ci * lc:ci * lc + 1, :])
        h = jnp.concatenate([h[e * p:(e + 1) * p] * cdec[:, e:e + 1] + st[e * p:(e + 1) * p]
                             for e in range(SSD_HPG)], axis=0)
    h_s[...] = h
    y = y_intra + jnp.concatenate(inter, axis=0) * eg_x + x * dskip_ref[0]
    y = y * _silu(z_ref[0])
    y = y * lax.rsqrt(jnp.mean(y * y, axis=-1, keepdims=True) + EPS) * ng_ref[0]
    y_ref[0] = y.astype(y_ref.dtype)

    @pl.when(t == pl.num_programs(2) - 1)
    def _():
        hout_ref[0] = h_s[...].reshape(SSD_HPG, p, SSD_DSTATE)


def _ssd_core(proj, conv_hist, h0, conv_w, conv_b, dt_bias, a_log, d_skip, norm_g):
    b, t, _ = proj.shape
    lc = min(CHUNK, t)
    tt = min(ROW_TILE, t)
    g_n = SSD_GROUPS
    gs = SSD_GS
    xblk = SSD_INNER // gs
    bblk = 2 * SSD_INNER // LANES
    cblk = bblk + g_n
    dblk = cblk + g_n

    def per_group(v):
        return jnp.pad(v.reshape(g_n, 1, SSD_HPG), ((0, 0), (0, 0), (0, LANES - SSD_HPG)))

    dskip_x = jnp.repeat(d_skip, SSD_HEADDIM).reshape(g_n, 1, gs)
    grow = pl.BlockSpec((1, 1, LANES), lambda i, g, s: (g, 0, 0))
    cwb = conv_b.reshape(1, SSD_CONV_DIM)
    in_specs = [
        pl.BlockSpec((1, tt, gs), lambda i, g, s: (i, s, g)),
        pl.BlockSpec((1, tt, gs), lambda i, g, s: (i, s, xblk + g)),
        pl.BlockSpec((1, tt, LANES), lambda i, g, s: (i, s, bblk + g)),
        pl.BlockSpec((1, tt, LANES), lambda i, g, s: (i, s, cblk + g)),
        pl.BlockSpec((1, tt, LANES), lambda i, g, s: (i, s, dblk + g)),
        pl.BlockSpec((1, CONV_W - 1, gs), lambda i, g, s: (i, 0, g)),
        pl.BlockSpec((1, CONV_W - 1, LANES), lambda i, g, s: (i, 0, SSD_INNER // LANES + g)),
        pl.BlockSpec((1, CONV_W - 1, LANES), lambda i, g, s: (i, 0, SSD_INNER // LANES + g_n + g)),
        pl.BlockSpec((CONV_W, gs), lambda i, g, s: (0, g)),
        pl.BlockSpec((CONV_W, LANES), lambda i, g, s: (0, SSD_INNER // LANES + g)),
        pl.BlockSpec((CONV_W, LANES), lambda i, g, s: (0, SSD_INNER // LANES + g_n + g)),
        pl.BlockSpec((1, gs), lambda i, g, s: (0, g)),
        pl.BlockSpec((1, LANES), lambda i, g, s: (0, SSD_INNER // LANES + g)),
        pl.BlockSpec((1, LANES), lambda i, g, s: (0, SSD_INNER // LANES + g_n + g)),
        pl.BlockSpec((1, SSD_HPG, SSD_HEADDIM, SSD_DSTATE), lambda i, g, s: (i, g, 0, 0)),
        grow, grow,
        pl.BlockSpec((1, 1, gs), lambda i, g, s: (g, 0, 0)),
        pl.BlockSpec((1, 1, gs), lambda i, g, s: (g, 0, 0)),
    ]
    return pl.pallas_call(
        functools.partial(_ssd_kernel, lc),
        grid=(b, g_n, t // tt),
        in_specs=in_specs,
        out_specs=[pl.BlockSpec((1, tt, gs), lambda i, g, s: (i, s, g)),
                   pl.BlockSpec((1, SSD_HPG, SSD_HEADDIM, SSD_DSTATE), lambda i, g, s: (i, g, 0, 0))],
        out_shape=[jax.ShapeDtypeStruct((b, t, SSD_INNER), BF16),
                   jax.ShapeDtypeStruct((b, SSD_HEADS, SSD_HEADDIM, SSD_DSTATE), F32)],
        scratch_shapes=[pltpu.VMEM((CONV_PAD + tt, gs), F32),
                        pltpu.VMEM((CONV_PAD + tt, LANES), F32),
                        pltpu.VMEM((CONV_PAD + tt, LANES), F32),
                        pltpu.VMEM((SSD_HPG * SSD_HEADDIM, SSD_DSTATE), F32)],
        compiler_params=_cparams(("arbitrary", "arbitrary", "arbitrary")),
        name="ssd_core",
    )(proj, proj, proj, proj, proj, conv_hist, conv_hist, conv_hist, conv_w, conv_w, conv_w,
      cwb, cwb, cwb, h0, per_group(a_log), per_group(dt_bias), dskip_x,
      norm_g.reshape(g_n, 1, gs))


def _outproj_kernel(o_ref, x_ref, g1_ref, w_ref, n2_ref, sc_ref, sh_ref, wr_ref, br_ref,
                    x1_ref, h2_ref, ri_ref, rw_ref, cnt_ref, cnt_s):
    bb, tt, d = x_ref.shape
    tm = bb * tt
    step = pl.program_id(0) * pl.num_programs(1) + pl.program_id(1)

    @pl.when(step == 0)
    def _():
        cnt_s[...] = jnp.zeros_like(cnt_s)

    out = jnp.dot(o_ref[...].reshape(tm, o_ref.shape[-1]), w_ref[...], preferred_element_type=F32)
    x1 = x_ref[...] + g1_ref[...] * out.reshape(bb, tt, d)
    x1_ref[...] = x1
    y = x1 * lax.rsqrt(jnp.mean(x1 * x1, axis=-1, keepdims=True) + EPS) * n2_ref[...]
    h2 = (y * (1.0 + sc_ref[...]) + sh_ref[...]).reshape(tm, d)
    h2_ref[...] = h2

    logits = _mm(h2, wr_ref[...]) + br_ref[...]
    lane = lax.broadcasted_iota(I32, (tm, LANES), 1)
    lane_f = lane.astype(F32)
    neg = -jnp.inf
    gl = jnp.where(lane < MOE_GROUPS, logits, neg)
    ge = jnp.exp(gl - jnp.max(gl, axis=-1, keepdims=True))
    grp_p = ge / jnp.sum(ge, axis=-1, keepdims=True)
    gp = jnp.max(grp_p, axis=-1, keepdims=True)
    gi = jnp.min(jnp.where(grp_p == gp, lane_f, float(LANES)), axis=-1, keepdims=True).astype(I32)
    lo = MOE_GROUPS + gi * MOE_PER_GROUP
    emask = (lane >= lo) & (lane < lo + MOE_PER_GROUP)
    sel = jnp.where(emask, logits, neg)
    se = jnp.exp(sel - jnp.max(sel, axis=-1, keepdims=True))
    p = jnp.where(emask, se / jnp.sum(se, axis=-1, keepdims=True), -1.0)
    v1 = jnp.max(p, axis=-1, keepdims=True)
    i1 = jnp.min(jnp.where(p == v1, lane_f, float(LANES)), axis=-1, keepdims=True).astype(I32)
    p2 = jnp.where(lane == i1, -1.0, p)
    v2 = jnp.max(p2, axis=-1, keepdims=True)
    i2 = jnp.min(jnp.where(p2 == v2, lane_f, float(LANES)), axis=-1, keepdims=True).astype(I32)
    den = v1 + v2
    w1 = gp * v1 / den
    w2 = gp * v2 / den
    e1 = i1 - MOE_GROUPS
    e2 = i2 - MOE_GROUPS

    hit1 = lane == e1
    hit2 = lane == e2
    onehot = jnp.where(hit1 | hit2, 1.0, 0.0)
    r = lax.broadcasted_iota(I32, (tm, tm), 0)
    c = lax.broadcasted_iota(I32, (tm, tm), 1)
    before = _mm(jnp.where(r > c, 1.0, 0.0), onehot) + cnt_s[...]
    r1 = jnp.sum(jnp.where(hit1, before, 0.0), axis=-1, keepdims=True).astype(I32)
    r2 = jnp.sum(jnp.where(hit2, before, 0.0), axis=-1, keepdims=True).astype(I32)
    cnt_s[...] = cnt_s[...] + jnp.sum(onehot, axis=0, keepdims=True)
    cnt_ref[...] = cnt_s[...].astype(I32)
    ri_ref[...] = jnp.where(lane == 0, e1, jnp.where(lane == 1, e2, jnp.where(lane == 2, r1, jnp.where(lane == 3, r2, 0))))
    rw_ref[...] = jnp.where(lane == 0, w1, jnp.where(lane == 1, w2, 0.0))


def _out_proj_route(o, x, mod, w_bf16, norm2_g, w_router, b_router):
    b, t, d = x.shape
    kdim = o.shape[-1]
    bb, tt = _tiles(b, t)
    tm = bb * tt
    nt = t // tt
    tok = lambda i, j: (i * nt + j, 0)
    modspec = lambda idx: pl.BlockSpec((bb, 1, d), lambda i, j: (i, 0, idx))
    return pl.pallas_call(
        _outproj_kernel,
        grid=(b // bb, nt),
        in_specs=[pl.BlockSpec((bb, tt, kdim), lambda i, j: (i, j, 0)),
                  pl.BlockSpec((bb, tt, d), lambda i, j: (i, j, 0)),
                  modspec(2),
                  pl.BlockSpec((kdim, d), lambda i, j: (0, 0)),
                  pl.BlockSpec((1, d), lambda i, j: (0, 0)),
                  modspec(4), modspec(3),
                  pl.BlockSpec((d, LANES), lambda i, j: (0, 0)),
                  pl.BlockSpec((1, LANES), lambda i, j: (0, 0))],
        out_specs=[pl.BlockSpec((bb, tt, d), lambda i, j: (i, j, 0)),
                   pl.BlockSpec((tm, d), tok),
                   pl.BlockSpec((tm, LANES), tok),
                   pl.BlockSpec((tm, LANES), tok),
                   pl.BlockSpec((1, LANES), lambda i, j: (0, 0))],
        out_shape=[jax.ShapeDtypeStruct((b, t, d), F32),
                   jax.ShapeDtypeStruct((b * t, d), F32),
                   jax.ShapeDtypeStruct((b * t, LANES), I32),
                   jax.ShapeDtypeStruct((b * t, LANES), F32),
                   jax.ShapeDtypeStruct((1, LANES), I32)],
        scratch_shapes=[pltpu.VMEM((1, LANES), F32)],
        compiler_params=_cparams(("arbitrary", "arbitrary")),
        name="out_proj_route",
    )(o, x, mod, w_bf16, norm2_g.reshape(1, d), mod, mod, w_router, b_router)


def _dispatch_kernel(dest_ref, h2_ref, xb_in_ref, xb_ref, sem):
    del xb_in_ref
    tm = h2_ref.shape[0]

    def copy(r, k):
        return pltpu.make_async_copy(h2_ref.at[pl.ds(r, 1), :],
                                     xb_ref.at[pl.ds(dest_ref[0, 0, MOE_TOPK * r + k], 1), :], sem)

    for r in range(tm):
        for k in range(MOE_TOPK):
            copy(r, k).start()
    for r in range(tm):
        for k in range(MOE_TOPK):
            copy(r, k).wait()


def _dispatch(h2, dest_tiles, cap):
    n, d = h2.shape
    nt, _, per = dest_tiles.shape
    tm = per // MOE_TOPK
    return pl.pallas_call(
        _dispatch_kernel,
        grid=(nt,),
        in_specs=[pl.BlockSpec((1, 1, per), lambda i: (i, 0, 0), memory_space=pltpu.SMEM),
                  pl.BlockSpec((tm, d), lambda i: (i, 0)),
                  pl.BlockSpec(memory_space=pl.ANY)],
        out_specs=pl.BlockSpec(memory_space=pl.ANY),
        out_shape=jax.ShapeDtypeStruct((cap, d), F32),
        scratch_shapes=[pltpu.SemaphoreType.DMA],
        input_output_aliases={2: 0},
        compiler_params=_cparams(("arbitrary",)),
        name="moe_dispatch",
    )(dest_tiles, h2, jnp.zeros((cap, d), F32))


def _expert_kernel(be_ref, nu_ref, x_ref, wg_ref, wu_ref, wd_ref, y_ref):
    used = pl.program_id(0) < nu_ref[0]

    @pl.when(used)
    def _():
        x = x_ref[...]
        hid = _silu(_mm(x, wg_ref[0, 0])) * _mm(x, wu_ref[0, 0])
        y_ref[...] = _mm(hid, wd_ref[0, 0])

    @pl.when(jnp.logical_not(used))
    def _():
        y_ref[...] = jnp.zeros_like(y_ref)


def _experts(xb, block_e, n_used, layer, w_gate, w_up, w_down):
    cap, d = xb.shape
    nb = cap // EXPERT_ROWS
    last = lambda i, nu: jnp.maximum(jnp.minimum(i, nu[0] - 1), 0)
    blk = lambda i, be, nu: (last(i, nu), 0)
    wsel = lambda i, be, nu: (layer, be[last(i, nu)], 0, 0)
    return pl.pallas_call(
        _expert_kernel,
        grid_spec=pltpu.PrefetchScalarGridSpec(
            num_scalar_prefetch=2,
            grid=(nb,),
            in_specs=[pl.BlockSpec((EXPERT_ROWS, d), blk),
                      pl.BlockSpec((1, 1, d, D_EXPERT), wsel),
                      pl.BlockSpec((1, 1, d, D_EXPERT), wsel),
                      pl.BlockSpec((1, 1, D_EXPERT, d), wsel)],
            out_specs=pl.BlockSpec((EXPERT_ROWS, d), lambda i, be, nu: (i, 0))),
        out_shape=jax.ShapeDtypeStruct((cap, d), F32),
        compiler_params=_cparams(("arbitrary",)),
        name="moe_experts",
    )(block_e, n_used, xb, w_gate, w_up, w_down)


def _combine_kernel(final, dest_ref, x1_ref, rw_ref, g2_ref, fg_ref, yb_ref, o_ref, buf_s, sem):
    bb, tt, d = x1_ref.shape
    tm = bb * tt

    def copy(r, k):
        return pltpu.make_async_copy(yb_ref.at[pl.ds(dest_ref[0, 0, MOE_TOPK * r + k], 1), :],
                                     buf_s.at[k, pl.ds(r, 1), :], sem)

    for r in range(tm):
        for k in range(MOE_TOPK):
            copy(r, k).start()
    for r in range(tm):
        for k in range(MOE_TOPK):
            copy(r, k).wait()
    rw = rw_ref[...]
    moe = buf_s[0] * rw[:, 0:1] + buf_s[1] * rw[:, 1:2]
    x2 = x1_ref[...] + g2_ref[...] * moe.reshape(bb, tt, d)
    if final:
        x2 = x2 * lax.rsqrt(jnp.mean(x2 * x2, axis=-1, keepdims=True) + EPS) * fg_ref[...]
    o_ref[...] = x2


def _combine(x1, route_w, mod, yb, dest_tiles, final_g, final):
    b, t, d = x1.shape
    bb, tt = _tiles(b, t)
    tm = bb * tt
    nt = t // tt
    per = dest_tiles.shape[-1]
    return pl.pallas_call(
        functools.partial(_combine_kernel, final),
        grid=(b // bb, nt),
        in_specs=[pl.BlockSpec((1, 1, per), lambda i, j: (i * nt + j, 0, 0), memory_space=pltpu.SMEM),
                  pl.BlockSpec((bb, tt, d), lambda i, j: (i, j, 0)),
                  pl.BlockSpec((tm, LANES), lambda i, j: (i * nt + j, 0)),
                  pl.BlockSpec((bb, 1, d), lambda i, j: (i, 0, 5)),
                  pl.BlockSpec((1, d), lambda i, j: (0, 0)),
                  pl.BlockSpec(memory_space=pl.ANY)],
        out_specs=pl.BlockSpec((bb, tt, d), lambda i, j: (i, j, 0)),
        out_shape=jax.ShapeDtypeStruct((b, t, d), F32),
        scratch_shapes=[pltpu.VMEM((MOE_TOPK, tm, d), F32), pltpu.SemaphoreType.DMA],
        compiler_params=_cparams(("arbitrary", "arbitrary")),
        name="moe_combine",
    )(dest_tiles, x1, route_w, mod, final_g.reshape(1, d), yb)


def _moe(layer, x1, h2, route_i, route_w, counts, mod, w_gate, w_up, w_down, final_g, final):
    b, t, d = x1.shape
    n = b * t
    bb, tt = _tiles(b, t)
    tm = bb * tt
    n_asg = n * MOE_TOPK
    nb = (n_asg + N_EXPERTS * (EXPERT_ROWS - 1) + EXPERT_ROWS - 1) // EXPERT_ROWS
    cap = nb * EXPERT_ROWS
    cnt = counts[0, :N_EXPERTS]
    padded = (cnt + EXPERT_ROWS - 1) // EXPERT_ROWS * EXPERT_ROWS
    ends = jnp.cumsum(padded)
    pstart = ends - padded
    eid = route_i[:, :MOE_TOPK]
    rank = route_i[:, MOE_TOPK:2 * MOE_TOPK]
    first_slot = jnp.sum(jnp.where(eid[..., None] == jnp.arange(N_EXPERTS, dtype=I32), pstart, 0), axis=-1)
    dest_tiles = (first_slot + rank).astype(I32).reshape(n // tm, 1, tm * MOE_TOPK)
    blk_start = jnp.arange(nb, dtype=I32) * EXPERT_ROWS
    block_e = jnp.minimum(jnp.sum(blk_start[:, None] >= ends[None, :], axis=-1), N_EXPERTS - 1).astype(I32)
    n_used = (ends[-1:] // EXPERT_ROWS).astype(I32)
    xb = _dispatch(h2, dest_tiles, cap)
    yb = _experts(xb, block_e, n_used, layer, w_gate, w_up, w_down)
    return _combine(x1, route_w, mod, yb, dest_tiles, final_g, final)


def _pad_cols(w, n):
    return jnp.pad(w, ((0, 0), (0, n - w.shape[1])))


def _trunk(x, mods, gdn_conv, gdn_ssm, ssd_conv, ssd_ssm, w):
    b, t, d = x.shape
    gdn_main = GDN_CONV_DIM + GDN_VD
    w_in = jnp.concatenate([w['gdn_w_in'][0][:, :gdn_main], _pad_cols(w['gdn_w_in'][0][:, gdn_main:], LANES)],
                           axis=1).astype(BF16)
    proj = _in_proj(x, w['norm1_g'][0], mods[0], 1, 0, w_in)
    o, gdn_state = _gdn_core(proj, gdn_conv[0], gdn_ssm[0], w['gdn_conv_w'][0], w['gdn_A_log'][0],
                             w['gdn_dt_bias'][0], w['gdn_norm_g'][0])
    gdn_hist = proj[:, t - (CONV_W - 1):, :GDN_CONV_DIM]
    x = _layer_tail(0, o, x, mods[0], w['gdn_w_out'][0], w, False)
    ssd_main = SSD_INNER + SSD_CONV_DIM
    dt_cols = [_pad_cols(w['ssd_w_in'][0][:, ssd_main + g * SSD_HPG: ssd_main + (g + 1) * SSD_HPG], LANES)
               for g in range(SSD_GROUPS)]
    w_in = jnp.concatenate([w['ssd_w_in'][0][:, :ssd_main]] + dt_cols, axis=1).astype(BF16)
    proj = _in_proj(x, w['norm1_g'][1], mods[1], 1, 0, w_in)
    y, ssd_state = _ssd_core(proj, ssd_conv[0], ssd_ssm[0], w['ssd_conv_w'][0], w['ssd_conv_b'][0],
                             w['ssd_dt_bias'][0], w['ssd_A_log'][0], w['ssd_D'][0], w['ssd_norm_g'][0])
    ssd_hist = proj[:, t - (CONV_W - 1):, SSD_INNER:ssd_main]
    y_out = _layer_tail(1, y, x, mods[1], w['ssd_w_out'][0], w, True)
    return y_out, gdn_hist[None], gdn_state[None], ssd_hist[None], ssd_state[None]


def _layer_tail(i, mixed, x, mod, w_out, w, final):
    d = x.shape[-1]
    w_router = _pad_cols(jnp.concatenate([w['moe_w_group'][i], w['moe_w_expert'][i]], axis=1), LANES)
    b_router = _pad_cols(jnp.concatenate([w['moe_b_group'][i], w['moe_b_expert'][i]]).reshape(1, -1), LANES)
    x1, h2, route_i, route_w, counts = _out_proj_route(mixed, x, mod, w_out.astype(BF16), w['norm2_g'][i],
                                                       w_router, b_router)
    del d
    return _moe(i, x1, h2, route_i, route_w, counts, mod, w['moe_w_gate'], w['moe_w_up'], w['moe_w_down'],
                w['final_norm_g'], final)


def kernel(x_prompt, x_sample, state_gdn_conv, state_gdn_ssm, state_ssd_conv, state_ssd_ssm, c_prompt, c_sample,
           norm1_g, norm2_g, ada_w, ada_b, gdn_w_in, gdn_conv_w, gdn_A_log, gdn_dt_bias, gdn_norm_g, gdn_w_out,
           ssd_w_in, ssd_conv_w, ssd_conv_b, ssd_dt_bias, ssd_A_log, ssd_D, ssd_norm_g, ssd_w_out,
           moe_w_group, moe_b_group, moe_w_expert, moe_b_expert, moe_w_gate, moe_w_up, moe_w_down, final_norm_g):
    w = {'norm1_g': norm1_g, 'norm2_g': norm2_g, 'gdn_w_in': gdn_w_in, 'gdn_conv_w': gdn_conv_w,
         'gdn_A_log': gdn_A_log, 'gdn_dt_bias': gdn_dt_bias, 'gdn_norm_g': gdn_norm_g, 'gdn_w_out': gdn_w_out,
         'ssd_w_in': ssd_w_in, 'ssd_conv_w': ssd_conv_w, 'ssd_conv_b': ssd_conv_b, 'ssd_dt_bias': ssd_dt_bias,
         'ssd_A_log': ssd_A_log, 'ssd_D': ssd_D, 'ssd_norm_g': ssd_norm_g, 'ssd_w_out': ssd_w_out,
         'moe_w_group': moe_w_group, 'moe_b_group': moe_b_group, 'moe_w_expert': moe_w_expert,
         'moe_b_expert': moe_b_expert, 'moe_w_gate': moe_w_gate, 'moe_w_up': moe_w_up, 'moe_w_down': moe_w_down,
         'final_norm_g': final_norm_g}
    bp = x_prompt.shape[0]
    bs = x_sample.shape[0]
    dt_ = x_prompt.dtype
    n_seq = bp + bs
    n_pad = -n_seq % SUBLANES
    c_all = jnp.concatenate([c_prompt, c_sample, jnp.zeros((n_pad, c_prompt.shape[1]), dt_)], axis=0)
    mod_all = _ada_mod(c_all, ada_w, ada_b)
    mods_p = [mod_all[l, :bp][:, None, :] for l in range(DEPTH)]
    mods_s = [mod_all[l, bp:n_seq][:, None, :] for l in range(DEPTH)]
    n_gdn = state_gdn_conv.shape[0]
    n_ssd = state_ssd_conv.shape[0]
    z_gc = jnp.zeros((n_gdn, bp) + state_gdn_conv.shape[2:], dt_)
    z_gs = jnp.zeros((n_gdn, bp) + state_gdn_ssm.shape[2:], dt_)
    z_sc = jnp.zeros((n_ssd, bp) + state_ssd_conv.shape[2:], dt_)
    z_ss = jnp.zeros((n_ssd, bp) + state_ssd_ssm.shape[2:], dt_)
    y_p, p_gc, p_gs, p_sc, p_ss = _trunk(x_prompt, mods_p, z_gc, z_gs, z_sc, z_ss, w)
    y_s, s_gc, s_gs, s_sc, s_ss = _trunk(x_sample, mods_s, state_gdn_conv, state_gdn_ssm,
                                         state_ssd_conv, state_ssd_ssm, w)
    return (y_p, y_s, p_gc, p_gs, p_sc, p_ss, s_gc, s_gs, s_sc, s_ss)
```

```python
import functools

import jax
import jax.numpy as jnp
from jax import lax
from jax.experimental import pallas as pl
from jax.experimental.pallas import tpu as pltpu

F32 = jnp.float32
BF16 = jnp.bfloat16
I32 = jnp.int32

D_MODEL = 1024
DEPTH = 2
CHUNK = 64
CONV_W = 4
EPS = 1e-6
GDN_HEADS = 8
GDN_DK = 128
GDN_DV = 128
GDN_QK = GDN_HEADS * GDN_DK
GDN_VD = GDN_HEADS * GDN_DV
GDN_CONV_DIM = 2 * GDN_QK + GDN_VD
SSD_INNER = 2 * D_MODEL
SSD_HEADDIM = 64
SSD_HEADS = SSD_INNER // SSD_HEADDIM
SSD_GROUPS = 4
SSD_HPG = SSD_HEADS // SSD_GROUPS
SSD_DSTATE = 128
SSD_GS = SSD_INNER // SSD_GROUPS
SSD_CONV_DIM = SSD_INNER + 2 * SSD_GROUPS * SSD_DSTATE
MOE_GROUPS = 4
MOE_PER_GROUP = 8
N_EXPERTS = MOE_GROUPS * MOE_PER_GROUP
MOE_TOPK = 2
D_EXPERT = 512

LANES = 128
SUBLANES = 8
VMEM_LIMIT = 48 * 1024 * 1024

ROW_TILE = 256
EXPERT_ROWS = 512
CONV_PAD = SUBLANES


def _cparams(sem):
    return pltpu.CompilerParams(dimension_semantics=sem, vmem_limit_bytes=VMEM_LIMIT)


def _mm(a, b):
    return jnp.dot(a.astype(BF16), b.astype(BF16), preferred_element_type=F32)


def _mm_nt(a, b):
    return lax.dot_general(a.astype(BF16), b.astype(BF16), (((1,), (1,)), ((), ())),
                           preferred_element_type=F32)


def _mm_tn(a, b):
    return lax.dot_general(a.astype(BF16), b.astype(BF16), (((0,), (0,)), ((), ())),
                           preferred_element_type=F32)


def _mm_f32(a, b):
    return jnp.dot(a, b, preferred_element_type=F32, precision=lax.Precision.HIGHEST)


def _silu(x):
    return x * jax.nn.sigmoid(x)


def _softplus(x):
    return jnp.maximum(x, 0.0) + jnp.log1p(jnp.exp(-jnp.abs(x)))


def _tiles(b, t):
    if t >= ROW_TILE:
        assert t % ROW_TILE == 0
        return 1, ROW_TILE
    assert ROW_TILE % t == 0 and b % (ROW_TILE // t) == 0
    return ROW_TILE // t, t


def _ada_kernel(c_ref, w_ref, b_ref, o_ref):
    o_ref[0] = _mm(_silu(c_ref[...]), w_ref[0]) + b_ref[0]


def _ada_mod(c_all, ada_w, ada_b):
    bp, d = c_all.shape
    n = ada_w.shape[-1]
    tn = 1024
    return pl.pallas_call(
        _ada_kernel,
        grid=(DEPTH, n // tn),
        in_specs=[pl.BlockSpec((bp, d), lambda l, j: (0, 0)),
                  pl.BlockSpec((1, d, tn), lambda l, j: (l, 0, j)),
                  pl.BlockSpec((1, 1, tn), lambda l, j: (l, 0, j))],
        out_specs=pl.BlockSpec((1, bp, tn), lambda l, j: (l, 0, j)),
        out_shape=jax.ShapeDtypeStruct((DEPTH, bp, n), F32),
        compiler_params=_cparams(("arbitrary", "arbitrary")),
        name="ada_mod",
    )(c_all, ada_w, ada_b.reshape(DEPTH, 1, n))


def _inproj_kernel(x_ref, g_ref, sc_ref, sh_ref, w_ref, o_ref):
    bb, tt, d = x_ref.shape
    x = x_ref[...]
    y = x * lax.rsqrt(jnp.mean(x * x, axis=-1, keepdims=True) + EPS) * g_ref[...]
    h = y * (1.0 + sc_ref[...]) + sh_ref[...]
    o = _mm(h.reshape(bb * tt, d), w_ref[...])
    o_ref[...] = o.reshape(bb, tt, o.shape[-1])


def _in_proj(x, norm_g, mod, sc_idx, sh_idx, w_bf16):
    b, t, d = x.shape
    n = w_bf16.shape[1]
    bb, tt = _tiles(b, t)
    return pl.pallas_call(
        _inproj_kernel,
        grid=(b // bb, t // tt),
        in_specs=[pl.BlockSpec((bb, tt, d), lambda i, j: (i, j, 0)),
                  pl.BlockSpec((1, d), lambda i, j: (0, 0)),
                  pl.BlockSpec((bb, 1, d), lambda i, j: (i, 0, sc_idx)),
                  pl.BlockSpec((bb, 1, d), lambda i, j: (i, 0, sh_idx)),
                  pl.BlockSpec((d, n), lambda i, j: (0, 0))],
        out_specs=pl.BlockSpec((bb, tt, n), lambda i, j: (i, j, 0)),
        out_shape=jax.ShapeDtypeStruct((b, t, n), F32),
        compiler_params=_cparams(("arbitrary", "arbitrary")),
        name="in_proj",
    )(x, norm_g.reshape(1, d), mod, mod, w_bf16)


CONV_H0 = CONV_PAD - (CONV_W - 1)


def _conv_tile(xp_ref, x_tile, w_ref, bias=None):
    tt = x_tile.shape[0]
    h0 = CONV_H0
    xp_ref[CONV_PAD:CONV_PAD + tt, :] = x_tile
    y = xp_ref[h0:h0 + tt, :] * w_ref[0:1, :]
    for i in range(1, CONV_W):
        y = y + xp_ref[h0 + i:h0 + i + tt, :] * w_ref[i:i + 1, :]
    xp_ref[h0:CONV_PAD, :] = xp_ref[tt + h0:tt + CONV_PAD, :]
    if bias is not None:
        y = y + bias
    return y


def _split_bf16(a):
    hi = a.astype(BF16)
    return hi, (a - hi.astype(F32)).astype(BF16)


def _mm_split(a, b):
    ah, al = _split_bf16(a)
    bh, bl = _split_bf16(b)
    dot = functools.partial(jnp.dot, preferred_element_type=F32)
    return dot(ah, bh) + dot(ah, bl) + dot(al, bh)


def _gdn_kernel(lc, qp_ref, kp_ref, vp_ref, z_ref, gt_ref, hq_ref, hk_ref, hv_ref,
                wq_ref, wk_ref, wv_ref, s0_ref, alog_ref, dtb_ref, ng_ref,
                o_ref, sout_ref,
                xq_s, xk_s, xv_s, st_s):
    t = pl.program_id(1)
    tt = qp_ref.shape[1]
    first = t == 0
    n_chunks = tt // lc
    heads = range(GDN_HEADS)
    sls = [slice(h * GDN_DK, (h + 1) * GDN_DK) for h in heads]

    @pl.when(first)
    def _():
        st_s[...] = s0_ref[0]
        xq_s[CONV_H0:CONV_PAD, :] = hq_ref[0]
        xk_s[CONV_H0:CONV_PAD, :] = hk_ref[0]
        xv_s[CONV_H0:CONV_PAD, :] = hv_ref[0]

    gates = gt_ref[0]
    beta = jax.nn.sigmoid(gates)
    g = -jnp.exp(alog_ref[...]) * _softplus(gates + dtb_ref[...])
    shift = lc.bit_length() - 1
    r = lax.broadcasted_iota(I32, (tt, tt), 0)
    c = lax.broadcasted_iota(I32, (tt, tt), 1)
    same = lax.shift_right_logical(r, shift) == lax.shift_right_logical(c, shift)
    gam = _mm_f32(jnp.where(same & (r >= c), 1.0, 0.0), g)
    gend = _mm_f32(jnp.where(same, 1.0, 0.0), g)
    egam = jnp.exp(gam)
    eend = jnp.exp(gend - gam)
    rw = lax.broadcasted_iota(I32, (lc, LANES), 0)
    cw = lax.broadcasted_iota(I32, (lc, LANES), 1)
    incl = (rw >= cw) & (cw < lc)
    strict = (rw > cw) & (cw < lc)
    right = (cw >= lc) & (cw < 2 * lc)
    eye_right = jnp.where(cw == rw + lc, 1.0, 0.0)

    xq_s[CONV_PAD:CONV_PAD + tt, :] = qp_ref[0]
    xk_s[CONV_PAD:CONV_PAD + tt, :] = kp_ref[0]
    xv_s[CONV_PAD:CONV_PAD + tt, :] = vp_ref[0]

    def conv_head(xp_ref, w_ref, ci, sl):
        r0 = CONV_H0 + ci * lc
        y = xp_ref[r0:r0 + lc, sl] * w_ref[0:1, sl]
        for i in range(1, CONV_W):
            y = y + xp_ref[r0 + i:r0 + i + lc, sl] * w_ref[i:i + 1, sl]
        return _silu(y)

    def prepare(ci, h):
        qh = conv_head(xq_s, wq_ref, ci, sls[h])
        kh = conv_head(xk_s, wk_ref, ci, sls[h])
        vh = conv_head(xv_s, wv_ref, ci, sls[h])
        qh = qh * lax.rsqrt(jnp.sum(qh * qh, axis=-1, keepdims=True) + EPS) * (GDN_DK ** -0.5)
        kh = kh * lax.rsqrt(jnp.sum(kh * kh, axis=-1, keepdims=True) + EPS)
        return qh, kh, vh

    def recur(ci, qkv):
        rows = slice(ci * lc, (ci + 1) * lc)
        qs = [qkv[h][0] for h in heads]
        ks = [qkv[h][1] for h in heads]
        vs = [qkv[h][2] for h in heads]
        gam_c = gam[rows, :]
        gam_t = jnp.concatenate([gam_c, jnp.zeros((LANES - lc, LANES), F32)], axis=0).T
        b_cols = [beta[rows, h:h + 1] for h in heads]
        egs = [egam[rows, GDN_HEADS + h:GDN_HEADS + h + 1] for h in heads]
        dec_incl = [jnp.exp(jnp.where(incl, gam_c[:, GDN_HEADS + h:GDN_HEADS + h + 1]
                                      - gam_t[GDN_HEADS + h:GDN_HEADS + h + 1, :], -jnp.inf)) for h in heads]
        zrow = jnp.zeros((LANES - lc, GDN_DK), F32)
        qkk = [_mm_nt(jnp.concatenate([qs[h], ks[h]], axis=0), jnp.concatenate([ks[h], zrow], axis=0))
               for h in heads]
        qk = [qkk[h][:lc] * dec_incl[h] for h in heads]
        a = [b_cols[h] * jnp.where(strict, dec_incl[h], 0.0) * qkk[h][lc:] for h in heads]
        yield
        cs = [eye_right - m for m in a]
        n = 1
        while n < lc:
            nxt_cs = []
            for cm in cs:
                ch, cl = _split_bf16(cm)
                ph, pl_ = ch[:, :lc], cl[:, :lc]
                dot = functools.partial(jnp.dot, preferred_element_type=F32)
                nxt_cs.append(dot(ph, ch) + dot(ph, cl) + dot(pl_, ch) + jnp.where(right, cm, 0.0))
            cs = nxt_cs
            n *= 2
            yield
        pad = [jnp.zeros((n_rows, GDN_DV + GDN_DK), F32) for n_rows in (lc, LANES - 2 * lc) if n_rows]
        rhs = [jnp.concatenate([pad[0], jnp.concatenate([b_cols[h] * vs[h], (b_cols[h] * egs[h]) * ks[h]], axis=-1)]
                               + pad[1:], axis=0) for h in heads]
        x = [_mm(cs[h], rhs[h]) for h in heads]
        yield
        s = [st_s[h] for h in heads]
        both = [_mm(jnp.concatenate([x[h][:, GDN_DV:], qs[h] * egs[h]], axis=0), s[h]) for h in heads]
        w = [x[h][:, :GDN_DV] - both[h][:lc] for h in heads]
        o = [both[h][lc:] + _mm(qk[h][:, :lc], w[h]) for h in heads]
        yield
        cdec = jnp.exp(gend[ci * lc:ci * lc + 1, :])
        for h in heads:
            gl = GDN_HEADS + h
            k_end = ks[h] * eend[rows, gl:gl + 1]
            st_s[h] = cdec[:, gl:gl + 1] * s[h] + _mm_tn(k_end, w[h])
        yield
        for h in heads:
            on = o[h] * lax.rsqrt(jnp.mean(o[h] * o[h], axis=-1, keepdims=True) + EPS) * ng_ref[...]
            o_ref[0, rows, sls[h]] = (on * _silu(z_ref[0, rows, sls[h]])).astype(o_ref.dtype)

    ready = [prepare(0, h) for h in heads]
    for ci in range(n_chunks):
        todo = iter(heads if ci + 1 < n_chunks else ())
        nxt = []
        for _ in recur(ci, ready):
            h = next(todo, None)
            if h is not None:
                nxt.append(prepare(ci + 1, h))
        nxt.extend(prepare(ci + 1, h) for h in todo)
        ready = nxt

    xq_s[CONV_H0:CONV_PAD, :] = xq_s[tt + CONV_H0:tt + CONV_PAD, :]
    xk_s[CONV_H0:CONV_PAD, :] = xk_s[tt + CONV_H0:tt + CONV_PAD, :]
    xv_s[CONV_H0:CONV_PAD, :] = xv_s[tt + CONV_H0:tt + CONV_PAD, :]

    @pl.when(t == pl.num_programs(1) - 1)
    def _():
        sout_ref[0] = st_s[...]


def _gdn_core(proj, conv_hist, s0, conv_w, a_log, dt_bias, norm_g):
    b, t, _ = proj.shape
    lc = min(CHUNK, t)
    tt = min(ROW_TILE, t)
    nq = GDN_QK // GDN_QK
    lane_pad = jnp.zeros((LANES - 2 * GDN_HEADS,), F32)
    alog_row = jnp.concatenate([jnp.zeros((GDN_HEADS,), F32), a_log, lane_pad]).reshape(1, LANES)
    dtb_row = jnp.concatenate([jnp.zeros((GDN_HEADS,), F32), dt_bias, lane_pad]).reshape(1, LANES)
    gate_blk = (GDN_CONV_DIM + GDN_VD) // LANES
    col = lambda j: pl.BlockSpec((1, tt, GDN_QK), lambda i, s: (i, s, j))
    hist = lambda j: pl.BlockSpec((1, CONV_W - 1, GDN_QK), lambda i, s: (i, 0, j))
    cw = lambda j: pl.BlockSpec((CONV_W, GDN_QK), lambda i, s: (0, j))
    row = pl.BlockSpec((1, LANES), lambda i, s: (0, 0))
    state = pl.BlockSpec((1, GDN_HEADS, GDN_DK, GDN_DV), lambda i, s: (i, 0, 0, 0))
    del nq
    return pl.pallas_call(
        functools.partial(_gdn_kernel, lc),
        grid=(b, t // tt),
        in_specs=[col(0), col(1), col(2), col(3),
                  pl.BlockSpec((1, tt, LANES), lambda i, s: (i, s, gate_blk)),
                  hist(0), hist(1), hist(2), cw(0), cw(1), cw(2), state, row, row, row],
        out_specs=[pl.BlockSpec((1, tt, GDN_VD), lambda i, s: (i, s, 0)), state],
        out_shape=[jax.ShapeDtypeStruct((b, t, GDN_VD), BF16),
                   jax.ShapeDtypeStruct((b, GDN_HEADS, GDN_DK, GDN_DV), F32)],
        scratch_shapes=[pltpu.VMEM((CONV_PAD + tt, GDN_QK), F32)] * 3
        + [pltpu.VMEM((GDN_HEADS, GDN_DK, GDN_DV), F32)],
        compiler_params=_cparams(("arbitrary", "arbitrary")),
        name="gdn_core",
    )(proj, proj, proj, proj, proj, conv_hist, conv_hist, conv_hist, conv_w, conv_w, conv_w,
      s0, alog_row, dtb_row, norm_g.reshape(1, GDN_DV))


def _ssd_kernel(lc, z_ref, xp_ref, bp_ref, cp_ref, dt_ref, hx_ref, hb_ref, hc_ref,
                wx_ref, wb_ref, wc_ref, bx_ref, bb_ref, bc_ref, h0_ref,
                alog_ref, dtb_ref, dskip_ref, ng_ref,
                y_ref, hout_ref,
                xx_s, xb_s, xc_s, h_s):
    t = pl.program_id(2)
    tt = xp_ref.shape[1]
    first = t == 0
    p = SSD_HEADDIM

    @pl.when(first)
    def _():
        h_s[...] = h0_ref[0].reshape(SSD_HPG * p, SSD_DSTATE)
        xx_s[CONV_H0:CONV_PAD, :] = hx_ref[0]
        xb_s[CONV_H0:CONV_PAD, :] = hb_ref[0]
        xc_s[CONV_H0:CONV_PAD, :] = hc_ref[0]

    dt = _softplus(dt_ref[0] + dtb_ref[0])
    da = dt * (-jnp.exp(alog_ref[0]))

    shift = lc.bit_length() - 1
    r = lax.broadcasted_iota(I32, (tt, tt), 0)
    c = lax.broadcasted_iota(I32, (tt, tt), 1)
    same = lax.shift_right_logical(r, shift) == lax.shift_right_logical(c, shift)
    incl = same & (r >= c)
    gam = _mm_f32(jnp.where(incl, 1.0, 0.0), da)
    gend = _mm_f32(jnp.where(same, 1.0, 0.0), da)
    gam_t = gam.T
    low_half = lax.broadcasted_iota(I32, (tt, LANES), 1) < p
    bm = _silu(_conv_tile(xb_s, bp_ref[0], wb_ref, bb_ref[...]))
    cm = _silu(_conv_tile(xc_s, cp_ref[0], wc_ref, bc_ref[...]))
    x = _silu(_conv_tile(xx_s, xp_ref[0], wx_ref, bx_ref[...]))

    def per_column(m):
        cols = [jnp.broadcast_to(m[:, e:e + 1], (tt, LANES)) for e in range(SSD_HPG)]
        return jnp.concatenate([jnp.where(low_half, cols[2 * i], cols[2 * i + 1])
                                for i in range(SSD_HPG // 2)], axis=-1)

    eg_x = per_column(jnp.exp(gam))
    xdt = x * per_column(dt)
    xe = xdt * per_column(jnp.exp(gend - gam))
    cb = _mm_nt(cm, bm)
    parts = []
    for pr in range(SSD_HPG // 2):
        cols = slice(pr * LANES, (pr + 1) * LANES)
        lms = [jnp.exp(jnp.where(incl, gam[:, e:e + 1] - gam_t[e:e + 1, :], -jnp.inf)) for e in (2 * pr, 2 * pr + 1)]
        ys = [_mm(cb * lm, xdt[:, cols]) for lm in lms]
        parts.append(jnp.where(low_half, ys[0], ys[1]))
    y_intra = jnp.concatenate(parts, axis=-1)

    h = h_s[...]
    inter = []
    for ci in range(tt // lc):
        rs = slice(ci * lc, (ci + 1) * lc)
        inter.append(_mm_nt(cm[rs], h))
        st = _mm_tn(xe[rs], bm[rs])
        cdec = jnp.exp(gend[ci * lc:ci * lc + 1, :])
        h = jnp.concatenate([h[e * p:(e + 1) * p] * cdec[:, e:e + 1] + st[e * p:(e + 1) * p]
                             for e in range(SSD_HPG)], axis=0)
    h_s[...] = h
    y = y_intra + jnp.concatenate(inter, axis=0) * eg_x + x * dskip_ref[0]
    y = y * _silu(z_ref[0])
    y = y * lax.rsqrt(jnp.mean(y * y, axis=-1, keepdims=True) + EPS) * ng_ref[0]
    y_ref[0] = y.astype(y_ref.dtype)

    @pl.when(t == pl.num_programs(2) - 1)
    def _():
        hout_ref[0] = h_s[...].reshape(SSD_HPG, p, SSD_DSTATE)


def _ssd_core(proj, conv_hist, h0, conv_w, conv_b, dt_bias, a_log, d_skip, norm_g):
    b, t, _ = proj.shape
    lc = min(CHUNK, t)
    tt = min(ROW_TILE, t)
    g_n = SSD_GROUPS
    gs = SSD_GS
    xblk = SSD_INNER // gs
    bblk = 2 * SSD_INNER // LANES
    cblk = bblk + g_n
    dblk = cblk + g_n

    def per_group(v):
        return jnp.pad(v.reshape(g_n, 1, SSD_HPG), ((0, 0), (0, 0), (0, LANES - SSD_HPG)))

    dskip_x = jnp.repeat(d_skip, SSD_HEADDIM).reshape(g_n, 1, gs)
    grow = pl.BlockSpec((1, 1, LANES), lambda i, g, s: (g, 0, 0))
    cwb = conv_b.reshape(1, SSD_CONV_DIM)
    in_specs = [
        pl.BlockSpec((1, tt, gs), lambda i, g, s: (i, s, g)),
        pl.BlockSpec((1, tt, gs), lambda i, g, s: (i, s, xblk + g)),
        pl.BlockSpec((1, tt, LANES), lambda i, g, s: (i, s, bblk + g)),
        pl.BlockSpec((1, tt, LANES), lambda i, g, s: (i, s, cblk + g)),
        pl.BlockSpec((1, tt, LANES), lambda i, g, s: (i, s, dblk + g)),
        pl.BlockSpec((1, CONV_W - 1, gs), lambda i, g, s: (i, 0, g)),
        pl.BlockSpec((1, CONV_W - 1, LANES), lambda i, g, s: (i, 0, SSD_INNER // LANES + g)),
        pl.BlockSpec((1, CONV_W - 1, LANES), lambda i, g, s: (i, 0, SSD_INNER // LANES + g_n + g)),
        pl.BlockSpec((CONV_W, gs), lambda i, g, s: (0, g)),
        pl.BlockSpec((CONV_W, LANES), lambda i, g, s: (0, SSD_INNER // LANES + g)),
        pl.BlockSpec((CONV_W, LANES), lambda i, g, s: (0, SSD_INNER // LANES + g_n + g)),
        pl.BlockSpec((1, gs), lambda i, g, s: (0, g)),
        pl.BlockSpec((1, LANES), lambda i, g, s: (0, SSD_INNER // LANES + g)),
        pl.BlockSpec((1, LANES), lambda i, g, s: (0, SSD_INNER // LANES + g_n + g)),
        pl.BlockSpec((1, SSD_HPG, SSD_HEADDIM, SSD_DSTATE), lambda i, g, s: (i, g, 0, 0)),
        grow, grow,
        pl.BlockSpec((1, 1, gs), lambda i, g, s: (g, 0, 0)),
        pl.BlockSpec((1, 1, gs), lambda i, g, s: (g, 0, 0)),
    ]
    return pl.pallas_call(
        functools.partial(_ssd_kernel, lc),
        grid=(b, g_n, t // tt),
        in_specs=in_specs,
        out_specs=[pl.BlockSpec((1, tt, gs), lambda i, g, s: (i, s, g)),
                   pl.BlockSpec((1, SSD_HPG, SSD_HEADDIM, SSD_DSTATE), lambda i, g, s: (i, g, 0, 0))],
        out_shape=[jax.ShapeDtypeStruct((b, t, SSD_INNER), BF16),
                   jax.ShapeDtypeStruct((b, SSD_HEADS, SSD_HEADDIM, SSD_DSTATE), F32)],
        scratch_shapes=[pltpu.VMEM((CONV_PAD + tt, gs), F32),
                        pltpu.VMEM((CONV_PAD + tt, LANES), F32),
                        pltpu.VMEM((CONV_PAD + tt, LANES), F32),
                        pltpu.VMEM((SSD_HPG * SSD_HEADDIM, SSD_DSTATE), F32)],
        compiler_params=_cparams(("arbitrary", "arbitrary", "arbitrary")),
        name="ssd_core",
    )(proj, proj, proj, proj, proj, conv_hist, conv_hist, conv_hist, conv_w, conv_w, conv_w,
      cwb, cwb, cwb, h0, per_group(a_log), per_group(dt_bias), dskip_x,
      norm_g.reshape(g_n, 1, gs))


def _outproj_kernel(o_ref, x_ref, g1_ref, w_ref, n2_ref, sc_ref, sh_ref, wr_ref, br_ref,
                    x1_ref, h2_ref, ri_ref, rw_ref, cnt_ref, cnt_s):
    bb, tt, d = x_ref.shape
    tm = bb * tt
    step = pl.program_id(0) * pl.num_programs(1) + pl.program_id(1)

    @pl.when(step == 0)
    def _():
        cnt_s[...] = jnp.zeros_like(cnt_s)

    out = jnp.dot(o_ref[...].reshape(tm, o_ref.shape[-1]), w_ref[...], preferred_element_type=F32)
    x1 = x_ref[...] + g1_ref[...] * out.reshape(bb, tt, d)
    x1_ref[...] = x1
    y = x1 * lax.rsqrt(jnp.mean(x1 * x1, axis=-1, keepdims=True) + EPS) * n2_ref[...]
    h2 = (y * (1.0 + sc_ref[...]) + sh_ref[...]).reshape(tm, d)
    h2_ref[...] = h2

    logits = _mm(h2, wr_ref[...]) + br_ref[...]
    lane = lax.broadcasted_iota(I32, (tm, LANES), 1)
    lane_f = lane.astype(F32)
    neg = -jnp.inf
    gl = jnp.where(lane < MOE_GROUPS, logits, neg)
    ge = jnp.exp(gl - jnp.max(gl, axis=-1, keepdims=True))
    grp_p = ge / jnp.sum(ge, axis=-1, keepdims=True)
    gp = jnp.max(grp_p, axis=-1, keepdims=True)
    gi = jnp.min(jnp.where(grp_p == gp, lane_f, float(LANES)), axis=-1, keepdims=True).astype(I32)
    lo = MOE_GROUPS + gi * MOE_PER_GROUP
    emask = (lane >= lo) & (lane < lo + MOE_PER_GROUP)
    sel = jnp.where(emask, logits, neg)
    se = jnp.exp(sel - jnp.max(sel, axis=-1, keepdims=True))
    p = jnp.where(emask, se / jnp.sum(se, axis=-1, keepdims=True), -1.0)
    v1 = jnp.max(p, axis=-1, keepdims=True)
    i1 = jnp.min(jnp.where(p == v1, lane_f, float(LANES)), axis=-1, keepdims=True).astype(I32)
    p2 = jnp.where(lane == i1, -1.0, p)
    v2 = jnp.max(p2, axis=-1, keepdims=True)
    i2 = jnp.min(jnp.where(p2 == v2, lane_f, float(LANES)), axis=-1, keepdims=True).astype(I32)
    den = v1 + v2
    w1 = gp * v1 / den
    w2 = gp * v2 / den
    e1 = i1 - MOE_GROUPS
    e2 = i2 - MOE_GROUPS

    hit1 = lane == e1
    hit2 = lane == e2
    onehot = jnp.where(hit1 | hit2, 1.0, 0.0)
    r = lax.broadcasted_iota(I32, (tm, tm), 0)
    c = lax.broadcasted_iota(I32, (tm, tm), 1)
    before = _mm(jnp.where(r > c, 1.0, 0.0), onehot) + cnt_s[...]
    r1 = jnp.sum(jnp.where(hit1, before, 0.0), axis=-1, keepdims=True).astype(I32)
    r2 = jnp.sum(jnp.where(hit2, before, 0.0), axis=-1, keepdims=True).astype(I32)
    cnt_s[...] = cnt_s[...] + jnp.sum(onehot, axis=0, keepdims=True)
    cnt_ref[...] = cnt_s[...].astype(I32)
    ri_ref[...] = jnp.where(lane == 0, e1, jnp.where(lane == 1, e2, jnp.where(lane == 2, r1, jnp.where(lane == 3, r2, 0))))
    rw_ref[...] = jnp.where(lane == 0, w1, jnp.where(lane == 1, w2, 0.0))


def _out_proj_route(o, x, mod, w_bf16, norm2_g, w_router, b_router):
    b, t, d = x.shape
    kdim = o.shape[-1]
    bb, tt = _tiles(b, t)
    tm = bb * tt
    nt = t // tt
    tok = lambda i, j: (i * nt + j, 0)
    modspec = lambda idx: pl.BlockSpec((bb, 1, d), lambda i, j: (i, 0, idx))
    return pl.pallas_call(
        _outproj_kernel,
        grid=(b // bb, nt),
        in_specs=[pl.BlockSpec((bb, tt, kdim), lambda i, j: (i, j, 0)),
                  pl.BlockSpec((bb, tt, d), lambda i, j: (i, j, 0)),
                  modspec(2),
                  pl.BlockSpec((kdim, d), lambda i, j: (0, 0)),
                  pl.BlockSpec((1, d), lambda i, j: (0, 0)),
                  modspec(4), modspec(3),
                  pl.BlockSpec((d, LANES), lambda i, j: (0, 0)),
                  pl.BlockSpec((1, LANES), lambda i, j: (0, 0))],
        out_specs=[pl.BlockSpec((bb, tt, d), lambda i, j: (i, j, 0)),
                   pl.BlockSpec((tm, d), tok),
                   pl.BlockSpec((tm, LANES), tok),
                   pl.BlockSpec((tm, LANES), tok),
                   pl.BlockSpec((1, LANES), lambda i, j: (0, 0))],
        out_shape=[jax.ShapeDtypeStruct((b, t, d), F32),
                   jax.ShapeDtypeStruct((b * t, d), F32),
                   jax.ShapeDtypeStruct((b * t, LANES), I32),
                   jax.ShapeDtypeStruct((b * t, LANES), F32),
                   jax.ShapeDtypeStruct((1, LANES), I32)],
        scratch_shapes=[pltpu.VMEM((1, LANES), F32)],
        compiler_params=_cparams(("arbitrary", "arbitrary")),
        name="out_proj_route",
    )(o, x, mod, w_bf16, norm2_g.reshape(1, d), mod, mod, w_router, b_router)


def _dispatch_kernel(dest_ref, h2_ref, xb_in_ref, xb_ref, sem):
    del xb_in_ref
    tm = h2_ref.shape[0]

    def copy(r, k):
        return pltpu.make_async_copy(h2_ref.at[pl.ds(r, 1), :],
                                     xb_ref.at[pl.ds(dest_ref[0, 0, MOE_TOPK * r + k], 1), :], sem)

    for r in range(tm):
        for k in range(MOE_TOPK):
            copy(r, k).start()
    for r in range(tm):
        for k in range(MOE_TOPK):
            copy(r, k).wait()


def _dispatch(h2, dest_tiles, cap):
    n, d = h2.shape
    nt, _, per = dest_tiles.shape
    tm = per // MOE_TOPK
    return pl.pallas_call(
        _dispatch_kernel,
        grid=(nt,),
        in_specs=[pl.BlockSpec((1, 1, per), lambda i: (i, 0, 0), memory_space=pltpu.SMEM),
                  pl.BlockSpec((tm, d), lambda i: (i, 0)),
                  pl.BlockSpec(memory_space=pl.ANY)],
        out_specs=pl.BlockSpec(memory_space=pl.ANY),
        out_shape=jax.ShapeDtypeStruct((cap, d), F32),
        scratch_shapes=[pltpu.SemaphoreType.DMA],
        input_output_aliases={2: 0},
        compiler_params=_cparams(("arbitrary",)),
        name="moe_dispatch",
    )(dest_tiles, h2, jnp.zeros((cap, d), F32))


def _expert_kernel(be_ref, nu_ref, x_ref, wg_ref, wu_ref, wd_ref, y_ref):
    used = pl.program_id(0) < nu_ref[0]

    @pl.when(used)
    def _():
        x = x_ref[...]
        hid = _silu(_mm(x, wg_ref[0, 0])) * _mm(x, wu_ref[0, 0])
        y_ref[...] = _mm(hid, wd_ref[0, 0])

    @pl.when(jnp.logical_not(used))
    def _():
        y_ref[...] = jnp.zeros_like(y_ref)


def _experts(xb, block_e, n_used, layer, w_gate, w_up, w_down):
    cap, d = xb.shape
    nb = cap // EXPERT_ROWS
    last = lambda i, nu: jnp.maximum(jnp.minimum(i, nu[0] - 1), 0)
    blk = lambda i, be, nu: (last(i, nu), 0)
    wsel = lambda i, be, nu: (layer, be[last(i, nu)], 0, 0)
    return pl.pallas_call(
        _expert_kernel,
        grid_spec=pltpu.PrefetchScalarGridSpec(
            num_scalar_prefetch=2,
            grid=(nb,),
            in_specs=[pl.BlockSpec((EXPERT_ROWS, d), blk),
                      pl.BlockSpec((1, 1, d, D_EXPERT), wsel),
                      pl.BlockSpec((1, 1, d, D_EXPERT), wsel),
                      pl.BlockSpec((1, 1, D_EXPERT, d), wsel)],
            out_specs=pl.BlockSpec((EXPERT_ROWS, d), lambda i, be, nu: (i, 0))),
        out_shape=jax.ShapeDtypeStruct((cap, d), F32),
        compiler_params=_cparams(("arbitrary",)),
        name="moe_experts",
    )(block_e, n_used, xb, w_gate, w_up, w_down)


def _combine_kernel(final, dest_ref, x1_ref, rw_ref, g2_ref, fg_ref, yb_ref, o_ref, buf_s, sem):
    bb, tt, d = x1_ref.shape
    tm = bb * tt

    def copy(r, k):
        return pltpu.make_async_copy(yb_ref.at[pl.ds(dest_ref[0, 0, MOE_TOPK * r + k], 1), :],
                                     buf_s.at[k, pl.ds(r, 1), :], sem)

    for r in range(tm):
        for k in range(MOE_TOPK):
            copy(r, k).start()
    for r in range(tm):
        for k in range(MOE_TOPK):
            copy(r, k).wait()
    rw = rw_ref[...]
    moe = buf_s[0] * rw[:, 0:1] + buf_s[1] * rw[:, 1:2]
    x2 = x1_ref[...] + g2_ref[...] * moe.reshape(bb, tt, d)
    if final:
        x2 = x2 * lax.rsqrt(jnp.mean(x2 * x2, axis=-1, keepdims=True) + EPS) * fg_ref[...]
    o_ref[...] = x2


def _combine(x1, route_w, mod, yb, dest_tiles, final_g, final):
    b, t, d = x1.shape
    bb, tt = _tiles(b, t)
    tm = bb * tt
    nt = t // tt
    per = dest_tiles.shape[-1]
    return pl.pallas_call(
        functools.partial(_combine_kernel, final),
        grid=(b // bb, nt),
        in_specs=[pl.BlockSpec((1, 1, per), lambda i, j: (i * nt + j, 0, 0), memory_space=pltpu.SMEM),
                  pl.BlockSpec((bb, tt, d), lambda i, j: (i, j, 0)),
                  pl.BlockSpec((tm, LANES), lambda i, j: (i * nt + j, 0)),
                  pl.BlockSpec((bb, 1, d), lambda i, j: (i, 0, 5)),
                  pl.BlockSpec((1, d), lambda i, j: (0, 0)),
                  pl.BlockSpec(memory_space=pl.ANY)],
        out_specs=pl.BlockSpec((bb, tt, d), lambda i, j: (i, j, 0)),
        out_shape=jax.ShapeDtypeStruct((b, t, d), F32),
        scratch_shapes=[pltpu.VMEM((MOE_TOPK, tm, d), F32), pltpu.SemaphoreType.DMA],
        compiler_params=_cparams(("arbitrary", "arbitrary")),
        name="moe_combine",
    )(dest_tiles, x1, route_w, mod, final_g.reshape(1, d), yb)


def _moe(layer, x1, h2, route_i, route_w, counts, mod, w_gate, w_up, w_down, final_g, final):
    b, t, d = x1.shape
    n = b * t
    bb, tt = _tiles(b, t)
    tm = bb * tt
    n_asg = n * MOE_TOPK
    nb = (n_asg + N_EXPERTS * (EXPERT_ROWS - 1) + EXPERT_ROWS - 1) // EXPERT_ROWS
    cap = nb * EXPERT_ROWS
    cnt = counts[0, :N_EXPERTS]
    padded = (cnt + EXPERT_ROWS - 1) // EXPERT_ROWS * EXPERT_ROWS
    ends = jnp.cumsum(padded)
    pstart = ends - padded
    eid = route_i[:, :MOE_TOPK]
    rank = route_i[:, MOE_TOPK:2 * MOE_TOPK]
    first_slot = jnp.sum(jnp.where(eid[..., None] == jnp.arange(N_EXPERTS, dtype=I32), pstart, 0), axis=-1)
    dest_tiles = (first_slot + rank).astype(I32).reshape(n // tm, 1, tm * MOE_TOPK)
    blk_start = jnp.arange(nb, dtype=I32) * EXPERT_ROWS
    block_e = jnp.minimum(jnp.sum(blk_start[:, None] >= ends[None, :], axis=-1), N_EXPERTS - 1).astype(I32)
    n_used = (ends[-1:] // EXPERT_ROWS).astype(I32)
    xb = _dispatch(h2, dest_tiles, cap)
    yb = _experts(xb, block_e, n_used, layer, w_gate, w_up, w_down)
    return _combine(x1, route_w, mod, yb, dest_tiles, final_g, final)


def _pad_cols(w, n):
    return jnp.pad(w, ((0, 0), (0, n - w.shape[1])))


def _trunk(x, mods, gdn_conv, gdn_ssm, ssd_conv, ssd_ssm, w):
    b, t, d = x.shape
    gdn_main = GDN_CONV_DIM + GDN_VD
    w_in = jnp.concatenate([w['gdn_w_in'][0][:, :gdn_main], _pad_cols(w['gdn_w_in'][0][:, gdn_main:], LANES)],
                           axis=1).astype(BF16)
    proj = _in_proj(x, w['norm1_g'][0], mods[0], 1, 0, w_in)
    o, gdn_state = _gdn_core(proj, gdn_conv[0], gdn_ssm[0], w['gdn_conv_w'][0], w['gdn_A_log'][0],
                             w['gdn_dt_bias'][0], w['gdn_norm_g'][0])
    gdn_hist = proj[:, t - (CONV_W - 1):, :GDN_CONV_DIM]
    x = _layer_tail(0, o, x, mods[0], w['gdn_w_out'][0], w, False)
    ssd_main = SSD_INNER + SSD_CONV_DIM
    dt_cols = [_pad_cols(w['ssd_w_in'][0][:, ssd_main + g * SSD_HPG: ssd_main + (g + 1) * SSD_HPG], LANES)
               for g in range(SSD_GROUPS)]
    w_in = jnp.concatenate([w['ssd_w_in'][0][:, :ssd_main]] + dt_cols, axis=1).astype(BF16)
    proj = _in_proj(x, w['norm1_g'][1], mods[1], 1, 0, w_in)
    y, ssd_state = _ssd_core(proj, ssd_conv[0], ssd_ssm[0], w['ssd_conv_w'][0], w['ssd_conv_b'][0],
                             w['ssd_dt_bias'][0], w['ssd_A_log'][0], w['ssd_D'][0], w['ssd_norm_g'][0])
    ssd_hist = proj[:, t - (CONV_W - 1):, SSD_INNER:ssd_main]
    y_out = _layer_tail(1, y, x, mods[1], w['ssd_w_out'][0], w, True)
    return y_out, gdn_hist[None], gdn_state[None], ssd_hist[None], ssd_state[None]


def _layer_tail(i, mixed, x, mod, w_out, w, final):
    d = x.shape[-1]
    w_router = _pad_cols(jnp.concatenate([w['moe_w_group'][i], w['moe_w_expert'][i]], axis=1), LANES)
    b_router = _pad_cols(jnp.concatenate([w['moe_b_group'][i], w['moe_b_expert'][i]]).reshape(1, -1), LANES)
    x1, h2, route_i, route_w, counts = _out_proj_route(mixed, x, mod, w_out.astype(BF16), w['norm2_g'][i],
                                                       w_router, b_router)
    del d
    return _moe(i, x1, h2, route_i, route_w, counts, mod, w['moe_w_gate'], w['moe_w_up'], w['moe_w_down'],
                w['final_norm_g'], final)


def kernel(x_prompt, x_sample, state_gdn_conv, state_gdn_ssm, state_ssd_conv, state_ssd_ssm, c_prompt, c_sample,
           norm1_g, norm2_g, ada_w, ada_b, gdn_w_in, gdn_conv_w, gdn_A_log, gdn_dt_bias, gdn_norm_g, gdn_w_out,
           ssd_w_in, ssd_conv_w, ssd_conv_b, ssd_dt_bias, ssd_A_log, ssd_D, ssd_norm_g, ssd_w_out,
           moe_w_group, moe_b_group, moe_w_expert, moe_b_expert, moe_w_gate, moe_w_up, moe_w_down, final_norm_g):
    w = {'norm1_g': norm1_g, 'norm2_g': norm2_g, 'gdn_w_in': gdn_w_in, 'gdn_conv_w': gdn_conv_w,
         'gdn_A_log': gdn_A_log, 'gdn_dt_bias': gdn_dt_bias, 'gdn_norm_g': gdn_norm_g, 'gdn_w_out': gdn_w_out,
         'ssd_w_in': ssd_w_in, 'ssd_conv_w': ssd_conv_w, 'ssd_conv_b': ssd_conv_b, 'ssd_dt_bias': ssd_dt_bias,
         'ssd_A_log': ssd_A_log, 'ssd_D': ssd_D, 'ssd_norm_g': ssd_norm_g, 'ssd_w_out': ssd_w_out,
         'moe_w_group': moe_w_group, 'moe_b_group': moe_b_group, 'moe_w_expert': moe_w_expert,
         'moe_b_expert': moe_b_expert, 'moe_w_gate': moe_w_gate, 'moe_w_up': moe_w_up, 'moe_w_down': moe_w_down,
         'final_norm_g': final_norm_g}
    bp = x_prompt.shape[0]
    bs = x_sample.shape[0]
    dt_ = x_prompt.dtype
    n_seq = bp + bs
    n_pad = -n_seq % SUBLANES
    c_all = jnp.concatenate([c_prompt, c_sample, jnp.zeros((n_pad, c_prompt.shape[1]), dt_)], axis=0)
    mod_all = _ada_mod(c_all, ada_w, ada_b)
    mods_p = [mod_all[l, :bp][:, None, :] for l in range(DEPTH)]
    mods_s = [mod_all[l, bp:n_seq][:, None, :] for l in range(DEPTH)]
    n_gdn = state_gdn_conv.shape[0]
    n_ssd = state_ssd_conv.shape[0]
    z_gc = jnp.zeros((n_gdn, bp) + state_gdn_conv.shape[2:], dt_)
    z_gs = jnp.zeros((n_gdn, bp) + state_gdn_ssm.shape[2:], dt_)
    z_sc = jnp.zeros((n_ssd, bp) + state_ssd_conv.shape[2:], dt_)
    z_ss = jnp.zeros((n_ssd, bp) + state_ssd_ssm.shape[2:], dt_)
    y_p, p_gc, p_gs, p_sc, p_ss = _trunk(x_prompt, mods_p, z_gc, z_gs, z_sc, z_ss, w)
    y_s, s_gc, s_gs, s_sc, s_ss = _trunk(x_sample, mods_s, state_gdn_conv, state_gdn_ssm,
                                         state_ssd_conv, state_ssd_ssm, w)
    return (y_p, y_s, p_gc, p_gs, p_sc, p_ss, s_gc, s_gs, s_sc, s_ss)
```

```python
import functools

import jax
import jax.numpy as jnp
from jax import lax
from jax.experimental import pallas as pl
from jax.experimental.pallas import tpu as pltpu

F32 = jnp.float32
BF16 = jnp.bfloat16
I32 = jnp.int32

D_MODEL = 1024
DEPTH = 2
CHUNK = 64
CONV_W = 4
EPS = 1e-6
GDN_HEADS = 8
GDN_DK = 128
GDN_DV = 128
GDN_QK = GDN_HEADS * GDN_DK
GDN_VD = GDN_HEADS * GDN_DV
GDN_CONV_DIM = 2 * GDN_QK + GDN_VD
SSD_INNER = 2 * D_MODEL
SSD_HEADDIM = 64
SSD_HEADS = SSD_INNER // SSD_HEADDIM
SSD_GROUPS = 4
SSD_HPG = SSD_HEADS // SSD_GROUPS
SSD_DSTATE = 128
SSD_GS = SSD_INNER // SSD_GROUPS
SSD_CONV_DIM = SSD_INNER + 2 * SSD_GROUPS * SSD_DSTATE
MOE_GROUPS = 4
MOE_PER_GROUP = 8
N_EXPERTS = MOE_GROUPS * MOE_PER_GROUP
MOE_TOPK = 2
D_EXPERT = 512

LANES = 128
SUBLANES = 8
VMEM_LIMIT = 48 * 1024 * 1024

ROW_TILE = 256
EXPERT_ROWS_MIN = 128
EXPERT_ROWS_MAX = 512
CONV_PAD = SUBLANES
SSD_GROUPS_PER_STEP = 4


def _cparams(sem):
    return pltpu.CompilerParams(dimension_semantics=sem, vmem_limit_bytes=VMEM_LIMIT)


def _mm(a, b):
    return jnp.dot(a.astype(BF16), b.astype(BF16), preferred_element_type=F32)


def _mm_nt(a, b):
    return lax.dot_general(a.astype(BF16), b.astype(BF16), (((1,), (1,)), ((), ())),
                           preferred_element_type=F32)


def _mm_tn(a, b):
    return lax.dot_general(a.astype(BF16), b.astype(BF16), (((0,), (0,)), ((), ())),
                           preferred_element_type=F32)


def _mm_f32(a, b):
    return jnp.dot(a, b, preferred_element_type=F32, precision=lax.Precision.HIGHEST)


def _silu(x):
    h = 0.5 * x
    return h + h * jnp.tanh(h)


def _softplus(x):
    return jnp.maximum(x, 0.0) + jnp.log1p(jnp.exp(-jnp.abs(x)))


def _tiles(b, t):
    if t >= ROW_TILE:
        assert t % ROW_TILE == 0
        return 1, ROW_TILE
    assert ROW_TILE % t == 0 and b % (ROW_TILE // t) == 0
    return ROW_TILE // t, t


def _ada_kernel(c_ref, w_ref, b_ref, o_ref):
    o_ref[0] = _mm(_silu(c_ref[...]), w_ref[0]) + b_ref[0]


def _ada_mod(c_all, ada_w, ada_b):
    bp, d = c_all.shape
    n = ada_w.shape[-1]
    tn = 1024
    return pl.pallas_call(
        _ada_kernel,
        grid=(DEPTH, n // tn),
        in_specs=[pl.BlockSpec((bp, d), lambda l, j: (0, 0)),
                  pl.BlockSpec((1, d, tn), lambda l, j: (l, 0, j)),
                  pl.BlockSpec((1, 1, tn), lambda l, j: (l, 0, j))],
        out_specs=pl.BlockSpec((1, bp, tn), lambda l, j: (l, 0, j)),
        out_shape=jax.ShapeDtypeStruct((DEPTH, bp, n), F32),
        compiler_params=_cparams(("arbitrary", "arbitrary")),
        name="ada_mod",
    )(c_all, ada_w, ada_b.reshape(DEPTH, 1, n))


def _inproj_kernel(x_ref, g_ref, sc_ref, sh_ref, w_ref, o_ref):
    bb, tt, d = x_ref.shape
    x = x_ref[...]
    y = x * lax.rsqrt(jnp.mean(x * x, axis=-1, keepdims=True) + EPS) * g_ref[...]
    h = y * (1.0 + sc_ref[...]) + sh_ref[...]
    o = _mm(h.reshape(bb * tt, d), w_ref[...])
    o_ref[...] = o.reshape(bb, tt, o.shape[-1])


def _in_proj(x, norm_g, mod, sc_idx, sh_idx, w_bf16):
    b, t, d = x.shape
    n = w_bf16.shape[1]
    bb, tt = _tiles(b, t)
    return pl.pallas_call(
        _inproj_kernel,
        grid=(b // bb, t // tt),
        in_specs=[pl.BlockSpec((bb, tt, d), lambda i, j: (i, j, 0)),
                  pl.BlockSpec((1, d), lambda i, j: (0, 0)),
                  pl.BlockSpec((bb, 1, d), lambda i, j: (i, 0, sc_idx)),
                  pl.BlockSpec((bb, 1, d), lambda i, j: (i, 0, sh_idx)),
                  pl.BlockSpec((d, n), lambda i, j: (0, 0))],
        out_specs=pl.BlockSpec((bb, tt, n), lambda i, j: (i, j, 0)),
        out_shape=jax.ShapeDtypeStruct((b, t, n), F32),
        compiler_params=_cparams(("arbitrary", "arbitrary")),
        name="in_proj",
    )(x, norm_g.reshape(1, d), mod, mod, w_bf16)


CONV_H0 = CONV_PAD - (CONV_W - 1)


def _chunk_cumsum(v, lc):
    tt = v.shape[0]
    shift = lc.bit_length() - 1
    r = lax.broadcasted_iota(I32, (tt, tt), 0)
    c = lax.broadcasted_iota(I32, (tt, tt), 1)
    same = lax.shift_right_logical(r, shift) == lax.shift_right_logical(c, shift)
    tri = jnp.where(same & (r >= c), 1.0, 0.0).astype(BF16)
    hi = v.astype(BF16)
    r1 = v - hi.astype(F32)
    mid = r1.astype(BF16)
    lo = (r1 - mid.astype(F32)).astype(BF16)
    dot = functools.partial(jnp.dot, preferred_element_type=F32)
    cum = dot(tri, hi) + dot(tri, mid) + dot(tri, lo)
    total = jnp.concatenate([jnp.broadcast_to(cum[e - 1:e, :], (lc, v.shape[1])) for e in range(lc, tt + 1, lc)],
                            axis=0)
    return cum, total


def _split_bf16(a):
    hi = a.astype(BF16)
    return hi, (a - hi.astype(F32)).astype(BF16)


def _mm_split(a, b):
    ah, al = _split_bf16(a)
    bh, bl = _split_bf16(b)
    dot = functools.partial(jnp.dot, preferred_element_type=F32)
    return dot(ah, bh) + dot(ah, bl) + dot(al, bh)


def _gdn_kernel(lc, qp_ref, kp_ref, vp_ref, z_ref, gt_ref, hq_ref, hk_ref, hv_ref,
                wq_ref, wk_ref, wv_ref, s0_ref, alog_ref, dtb_ref, ng_ref,
                o_ref, sout_ref,
                xq_s, xk_s, xv_s, st_s):
    t = pl.program_id(1)
    tt = qp_ref.shape[1]
    first = t == 0
    n_chunks = tt // lc
    heads = range(GDN_HEADS)
    sls = [slice(h * GDN_DK, (h + 1) * GDN_DK) for h in heads]

    @pl.when(first)
    def _():
        st_s[...] = s0_ref[0]
        xq_s[CONV_H0:CONV_PAD, :] = hq_ref[0]
        xk_s[CONV_H0:CONV_PAD, :] = hk_ref[0]
        xv_s[CONV_H0:CONV_PAD, :] = hv_ref[0]

    gates = gt_ref[0]
    beta = jax.nn.sigmoid(gates)
    g = -jnp.exp(alog_ref[...]) * _softplus(gates + dtb_ref[...])
    gam, gend = _chunk_cumsum(g, lc)
    egam = jnp.exp(gam)
    eend = jnp.exp(gend - gam)
    rw = lax.broadcasted_iota(I32, (lc, LANES), 0)
    cw = lax.broadcasted_iota(I32, (lc, LANES), 1)
    incl = (rw >= cw) & (cw < lc)
    strict = (rw > cw) & (cw < lc)
    right = (cw >= lc) & (cw < 2 * lc)
    eye_right = jnp.where(cw == rw + lc, 1.0, 0.0)

    xq_s[CONV_PAD:CONV_PAD + tt, :] = qp_ref[0]
    xk_s[CONV_PAD:CONV_PAD + tt, :] = kp_ref[0]
    xv_s[CONV_PAD:CONV_PAD + tt, :] = vp_ref[0]

    def conv_head(xp_ref, w_ref, ci, sl):
        r0 = CONV_H0 + ci * lc
        y = xp_ref[r0:r0 + lc, sl] * w_ref[0:1, sl]
        for i in range(1, CONV_W):
            y = y + xp_ref[r0 + i:r0 + i + lc, sl] * w_ref[i:i + 1, sl]
        return _silu(y)

    def prepare(ci, h):
        qh = conv_head(xq_s, wq_ref, ci, sls[h])
        kh = conv_head(xk_s, wk_ref, ci, sls[h])
        vh = conv_head(xv_s, wv_ref, ci, sls[h])
        qh = qh * lax.rsqrt(jnp.sum(qh * qh, axis=-1, keepdims=True) + EPS) * (GDN_DK ** -0.5)
        kh = kh * lax.rsqrt(jnp.sum(kh * kh, axis=-1, keepdims=True) + EPS)
        return qh, kh, vh

    def recur(ci, qkv):
        rows = slice(ci * lc, (ci + 1) * lc)
        qs = [qkv[h][0] for h in heads]
        ks = [qkv[h][1] for h in heads]
        vs = [qkv[h][2] for h in heads]
        gam_c = gam[rows, :]
        gam_t = jnp.concatenate([gam_c, jnp.zeros((LANES - lc, LANES), F32)], axis=0).T
        b_cols = [beta[rows, h:h + 1] for h in heads]
        egs = [egam[rows, GDN_HEADS + h:GDN_HEADS + h + 1] for h in heads]
        dec_incl = [jnp.exp(jnp.where(incl, gam_c[:, GDN_HEADS + h:GDN_HEADS + h + 1]
                                      - gam_t[GDN_HEADS + h:GDN_HEADS + h + 1, :], -jnp.inf)) for h in heads]
        zrow = jnp.zeros((LANES - lc, GDN_DK), F32)
        qkk = [_mm_nt(jnp.concatenate([qs[h], ks[h]], axis=0), jnp.concatenate([ks[h], zrow], axis=0))
               for h in heads]
        qk = [qkk[h][:lc] * dec_incl[h] for h in heads]
        a = [b_cols[h] * jnp.where(strict, dec_incl[h], 0.0) * qkk[h][lc:] for h in heads]
        yield
        cs = [eye_right - m for m in a]
        n = 1
        while n < lc:
            nxt_cs = []
            for cm in cs:
                ch, cl = _split_bf16(cm)
                ph, pl_ = ch[:, :lc], cl[:, :lc]
                dot = functools.partial(jnp.dot, preferred_element_type=F32)
                nxt_cs.append(dot(ph, ch) + dot(ph, cl) + dot(pl_, ch) + jnp.where(right, cm, 0.0))
            cs = nxt_cs
            n *= 2
            yield
        pad = [jnp.zeros((n_rows, GDN_DV + GDN_DK), F32) for n_rows in (lc, LANES - 2 * lc) if n_rows]
        rhs = [jnp.concatenate([pad[0], jnp.concatenate([b_cols[h] * vs[h], (b_cols[h] * egs[h]) * ks[h]], axis=-1)]
                               + pad[1:], axis=0) for h in heads]
        x = [_mm(cs[h], rhs[h]) for h in heads]
        yield
        s = [st_s[h] for h in heads]
        both = [_mm(jnp.concatenate([x[h][:, GDN_DV:], qs[h] * egs[h]], axis=0), s[h]) for h in heads]
        w = [x[h][:, :GDN_DV] - both[h][:lc] for h in heads]
        o = [both[h][lc:] + _mm(qk[h][:, :lc], w[h]) for h in heads]
        yield
        cdec = jnp.exp(gend[ci * lc:ci * lc + 1, :])
        for h in heads:
            gl = GDN_HEADS + h
            k_end = ks[h] * eend[rows, gl:gl + 1]
            st_s[h] = cdec[:, gl:gl + 1] * s[h] + _mm_tn(k_end, w[h])
        yield
        for h in heads:
            on = o[h] * lax.rsqrt(jnp.mean(o[h] * o[h], axis=-1, keepdims=True) + EPS) * ng_ref[...]
            o_ref[0, rows, sls[h]] = (on * _silu(z_ref[0, rows, sls[h]])).astype(o_ref.dtype)

    ready = [prepare(0, h) for h in heads]
    for ci in range(n_chunks):
        todo = iter(heads if ci + 1 < n_chunks else ())
        nxt = []
        for _ in recur(ci, ready):
            h = next(todo, None)
            if h is not None:
                nxt.append(prepare(ci + 1, h))
        nxt.extend(prepare(ci + 1, h) for h in todo)
        ready = nxt

    xq_s[CONV_H0:CONV_PAD, :] = xq_s[tt + CONV_H0:tt + CONV_PAD, :]
    xk_s[CONV_H0:CONV_PAD, :] = xk_s[tt + CONV_H0:tt + CONV_PAD, :]
    xv_s[CONV_H0:CONV_PAD, :] = xv_s[tt + CONV_H0:tt + CONV_PAD, :]

    @pl.when(t == pl.num_programs(1) - 1)
    def _():
        sout_ref[0] = st_s[...]


def _gdn_core(proj, conv_hist, s0, conv_w, a_log, dt_bias, norm_g):
    b, t, _ = proj.shape
    lc = min(CHUNK, t)
    tt = min(ROW_TILE, t)
    nq = GDN_QK // GDN_QK
    lane_pad = jnp.zeros((LANES - 2 * GDN_HEADS,), F32)
    alog_row = jnp.concatenate([jnp.zeros((GDN_HEADS,), F32), a_log, lane_pad]).reshape(1, LANES)
    dtb_row = jnp.concatenate([jnp.zeros((GDN_HEADS,), F32), dt_bias, lane_pad]).reshape(1, LANES)
    gate_blk = (GDN_CONV_DIM + GDN_VD) // LANES
    col = lambda j: pl.BlockSpec((1, tt, GDN_QK), lambda i, s: (i, s, j))
    hist = lambda j: pl.BlockSpec((1, CONV_W - 1, GDN_QK), lambda i, s: (i, 0, j))
    cw = lambda j: pl.BlockSpec((CONV_W, GDN_QK), lambda i, s: (0, j))
    row = pl.BlockSpec((1, LANES), lambda i, s: (0, 0))
    state = pl.BlockSpec((1, GDN_HEADS, GDN_DK, GDN_DV), lambda i, s: (i, 0, 0, 0))
    del nq
    return pl.pallas_call(
        functools.partial(_gdn_kernel, lc),
        grid=(b, t // tt),
        in_specs=[col(0), col(1), col(2), col(3),
                  pl.BlockSpec((1, tt, LANES), lambda i, s: (i, s, gate_blk)),
                  hist(0), hist(1), hist(2), cw(0), cw(1), cw(2), state, row, row, row],
        out_specs=[pl.BlockSpec((1, tt, GDN_VD), lambda i, s: (i, s, 0)), state],
        out_shape=[jax.ShapeDtypeStruct((b, t, GDN_VD), BF16),
                   jax.ShapeDtypeStruct((b, GDN_HEADS, GDN_DK, GDN_DV), F32)],
        scratch_shapes=[pltpu.VMEM((CONV_PAD + tt, GDN_QK), F32)] * 3
        + [pltpu.VMEM((GDN_HEADS, GDN_DK, GDN_DV), F32)],
        compiler_params=_cparams(("arbitrary", "arbitrary")),
        name="gdn_core",
    )(proj, proj, proj, proj, proj, conv_hist, conv_hist, conv_hist, conv_w, conv_w, conv_w,
      s0, alog_row, dtb_row, norm_g.reshape(1, GDN_DV))


def _ssd_kernel(lc, z_ref, xp_ref, bp_ref, cp_ref, dt_ref, hx_ref, hb_ref, hc_ref,
                wx_ref, wb_ref, wc_ref, bx_ref, bb_ref, bc_ref, h0_ref,
                alog_ref, dtb_ref, dskip_ref, ng_ref,
                y_ref, hout_ref,
                xx_s, xb_s, xc_s, h_s):
    t = pl.program_id(2)
    tt = xp_ref.shape[1]
    first = t == 0
    p = SSD_HEADDIM
    gs = SSD_GS
    hrows = SSD_HPG * p
    n_grp = alog_ref.shape[0]

    @pl.when(first)
    def _():
        h_s[...] = h0_ref[0].reshape(n_grp * hrows, SSD_DSTATE)
        xx_s[CONV_H0:CONV_PAD, :] = hx_ref[0]
        xb_s[CONV_H0:CONV_PAD, :] = hb_ref[0]
        xc_s[CONV_H0:CONV_PAD, :] = hc_ref[0]

    xx_s[CONV_PAD:CONV_PAD + tt, :] = xp_ref[0]
    xb_s[CONV_PAD:CONV_PAD + tt, :] = bp_ref[0]
    xc_s[CONV_PAD:CONV_PAD + tt, :] = cp_ref[0]

    def conv(xp_s, w_ref, b_ref, cols):
        y = xp_s[CONV_H0:CONV_H0 + tt, cols] * w_ref[0:1, cols]
        for i in range(1, CONV_W):
            y = y + xp_s[CONV_H0 + i:CONV_H0 + i + tt, cols] * w_ref[i:i + 1, cols]
        return _silu(y + b_ref[:, cols])

    shift = lc.bit_length() - 1
    r = lax.broadcasted_iota(I32, (tt, tt), 0)
    c = lax.broadcasted_iota(I32, (tt, tt), 1)
    incl = (lax.shift_right_logical(r, shift) == lax.shift_right_logical(c, shift)) & (r >= c)
    low_half = lax.broadcasted_iota(I32, (tt, LANES), 1) < p

    def per_column(m):
        cols = [jnp.broadcast_to(m[:, e:e + 1], (tt, LANES)) for e in range(SSD_HPG)]
        return jnp.concatenate([jnp.where(low_half, cols[2 * i], cols[2 * i + 1])
                                for i in range(SSD_HPG // 2)], axis=-1)

    def group(k):
        wide = slice(k * gs, (k + 1) * gs)
        lanes = slice(k * LANES, (k + 1) * LANES)
        dt = _softplus(dt_ref[0, :, lanes] + dtb_ref[k])
        gam, gend = _chunk_cumsum(dt * (-jnp.exp(alog_ref[k])), lc)
        yield
        bm = conv(xb_s, wb_ref, bb_ref, lanes)
        gam_t = gam.T
        cm = conv(xc_s, wc_ref, bc_ref, lanes)
        yield
        x = conv(xx_s, wx_ref, bx_ref, wide)
        yield
        eg_x = per_column(jnp.exp(gam))
        xdt = x * per_column(dt)
        xe = xdt * per_column(jnp.exp(gend - gam))
        cb = _mm_nt(cm, bm)
        yield
        parts = []
        for pr in range(SSD_HPG // 2):
            cols = slice(pr * LANES, (pr + 1) * LANES)
            lms = [jnp.exp(jnp.where(incl, gam[:, e:e + 1] - gam_t[e:e + 1, :], -jnp.inf))
                   for e in (2 * pr, 2 * pr + 1)]
            ys = [_mm(cb * lm, xdt[:, cols]) for lm in lms]
            parts.append(jnp.where(low_half, ys[0], ys[1]))
            yield
        y_intra = jnp.concatenate(parts, axis=-1)
        hs = slice(k * hrows, (k + 1) * hrows)
        h = h_s[hs, :]
        inter = []
        for ci in range(tt // lc):
            rs = slice(ci * lc, (ci + 1) * lc)
            inter.append(_mm_nt(cm[rs], h))
            st = _mm_tn(xe[rs], bm[rs])
            cdec = jnp.exp(gend[ci * lc:ci * lc + 1, :])
            h = jnp.concatenate([h[e * p:(e + 1) * p] * cdec[:, e:e + 1] + st[e * p:(e + 1) * p]
                                 for e in range(SSD_HPG)], axis=0)
            yield
        h_s[hs, :] = h
        y = y_intra + jnp.concatenate(inter, axis=0) * eg_x + x * dskip_ref[k]
        y = y * _silu(z_ref[0, :, wide])
        y = y * lax.rsqrt(jnp.mean(y * y, axis=-1, keepdims=True) + EPS) * ng_ref[k]
        y_ref[0, :, wide] = y.astype(y_ref.dtype)

    streams = [group(k) for k in range(n_grp)]
    while streams:
        streams = [s for s in streams if next(s, True) is None]

    xx_s[CONV_H0:CONV_PAD, :] = xx_s[tt + CONV_H0:tt + CONV_PAD, :]
    xb_s[CONV_H0:CONV_PAD, :] = xb_s[tt + CONV_H0:tt + CONV_PAD, :]
    xc_s[CONV_H0:CONV_PAD, :] = xc_s[tt + CONV_H0:tt + CONV_PAD, :]

    @pl.when(t == pl.num_programs(2) - 1)
    def _():
        hout_ref[0] = h_s[...].reshape(n_grp * SSD_HPG, p, SSD_DSTATE)


def _ssd_core(proj, conv_hist, h0, conv_w, conv_b, dt_bias, a_log, d_skip, norm_g):
    b, t, _ = proj.shape
    lc = min(CHUNK, t)
    tt = min(ROW_TILE, t)
    g_n = SSD_GROUPS
    gp = SSD_GROUPS_PER_STEP
    gs = SSD_GS
    wide = gp * gs
    narrow = gp * LANES
    xblk = SSD_INNER // wide
    bblk = 2 * SSD_INNER // narrow
    cblk = bblk + g_n // gp
    dblk = cblk + g_n // gp
    hist_b = SSD_INNER // narrow
    hist_c = hist_b + g_n // gp

    def per_group(v):
        return jnp.pad(v.reshape(g_n, 1, SSD_HPG), ((0, 0), (0, 0), (0, LANES - SSD_HPG)))

    dskip_x = jnp.repeat(d_skip, SSD_HEADDIM).reshape(g_n, 1, gs)
    grow = pl.BlockSpec((gp, 1, LANES), lambda i, g, s: (g, 0, 0))
    cwb = conv_b.reshape(1, SSD_CONV_DIM)
    state = pl.BlockSpec((1, gp * SSD_HPG, SSD_HEADDIM, SSD_DSTATE), lambda i, g, s: (i, g, 0, 0))
    in_specs = [
        pl.BlockSpec((1, tt, wide), lambda i, g, s: (i, s, g)),
        pl.BlockSpec((1, tt, wide), lambda i, g, s: (i, s, xblk + g)),
        pl.BlockSpec((1, tt, narrow), lambda i, g, s: (i, s, bblk + g)),
        pl.BlockSpec((1, tt, narrow), lambda i, g, s: (i, s, cblk + g)),
        pl.BlockSpec((1, tt, narrow), lambda i, g, s: (i, s, dblk + g)),
        pl.BlockSpec((1, CONV_W - 1, wide), lambda i, g, s: (i, 0, g)),
        pl.BlockSpec((1, CONV_W - 1, narrow), lambda i, g, s: (i, 0, hist_b + g)),
        pl.BlockSpec((1, CONV_W - 1, narrow), lambda i, g, s: (i, 0, hist_c + g)),
        pl.BlockSpec((CONV_W, wide), lambda i, g, s: (0, g)),
        pl.BlockSpec((CONV_W, narrow), lambda i, g, s: (0, hist_b + g)),
        pl.BlockSpec((CONV_W, narrow), lambda i, g, s: (0, hist_c + g)),
        pl.BlockSpec((1, wide), lambda i, g, s: (0, g)),
        pl.BlockSpec((1, narrow), lambda i, g, s: (0, hist_b + g)),
        pl.BlockSpec((1, narrow), lambda i, g, s: (0, hist_c + g)),
        state,
        grow, grow,
        pl.BlockSpec((gp, 1, gs), lambda i, g, s: (g, 0, 0)),
        pl.BlockSpec((gp, 1, gs), lambda i, g, s: (g, 0, 0)),
    ]
    return pl.pallas_call(
        functools.partial(_ssd_kernel, lc),
        grid=(b, g_n // gp, t // tt),
        in_specs=in_specs,
        out_specs=[pl.BlockSpec((1, tt, wide), lambda i, g, s: (i, s, g)), state],
        out_shape=[jax.ShapeDtypeStruct((b, t, SSD_INNER), BF16),
                   jax.ShapeDtypeStruct((b, SSD_HEADS, SSD_HEADDIM, SSD_DSTATE), F32)],
        scratch_shapes=[pltpu.VMEM((CONV_PAD + tt, wide), F32),
                        pltpu.VMEM((CONV_PAD + tt, narrow), F32),
                        pltpu.VMEM((CONV_PAD + tt, narrow), F32),
                        pltpu.VMEM((gp * SSD_HPG * SSD_HEADDIM, SSD_DSTATE), F32)],
        compiler_params=_cparams(("arbitrary", "arbitrary", "arbitrary")),
        name="ssd_core",
    )(proj, proj, proj, proj, proj, conv_hist, conv_hist, conv_hist, conv_w, conv_w, conv_w,
      cwb, cwb, cwb, h0, per_group(a_log), per_group(dt_bias), dskip_x,
      norm_g.reshape(g_n, 1, gs))


def _outproj_kernel(o_ref, x_ref, g1_ref, w_ref, n2_ref, sc_ref, sh_ref, wr_ref, br_ref,
                    x1_ref, h2_ref, ri_ref, rw_ref, cnt_ref, cnt_s):
    bb, tt, d = x_ref.shape
    tm = bb * tt
    step = pl.program_id(0) * pl.num_programs(1) + pl.program_id(1)

    @pl.when(step == 0)
    def _():
        cnt_s[...] = jnp.zeros_like(cnt_s)

    out = jnp.dot(o_ref[...].reshape(tm, o_ref.shape[-1]), w_ref[...], preferred_element_type=F32)
    x1 = x_ref[...] + g1_ref[...] * out.reshape(bb, tt, d)
    x1_ref[...] = x1
    y = x1 * lax.rsqrt(jnp.mean(x1 * x1, axis=-1, keepdims=True) + EPS) * n2_ref[...]
    h2 = (y * (1.0 + sc_ref[...]) + sh_ref[...]).reshape(tm, d)
    h2_ref[...] = h2

    logits = _mm(h2, wr_ref[...]) + br_ref[...]
    lane = lax.broadcasted_iota(I32, (tm, LANES), 1)
    lane_f = lane.astype(F32)
    neg = -jnp.inf
    gl = jnp.where(lane < MOE_GROUPS, logits, neg)
    ge = jnp.exp(gl - jnp.max(gl, axis=-1, keepdims=True))
    grp_p = ge / jnp.sum(ge, axis=-1, keepdims=True)
    gp = jnp.max(grp_p, axis=-1, keepdims=True)
    gi = jnp.min(jnp.where(grp_p == gp, lane_f, float(LANES)), axis=-1, keepdims=True).astype(I32)
    lo = MOE_GROUPS + gi * MOE_PER_GROUP
    emask = (lane >= lo) & (lane < lo + MOE_PER_GROUP)
    sel = jnp.where(emask, logits, neg)
    se = jnp.exp(sel - jnp.max(sel, axis=-1, keepdims=True))
    p = jnp.where(emask, se / jnp.sum(se, axis=-1, keepdims=True), -1.0)
    v1 = jnp.max(p, axis=-1, keepdims=True)
    i1 = jnp.min(jnp.where(p == v1, lane_f, float(LANES)), axis=-1, keepdims=True).astype(I32)
    p2 = jnp.where(lane == i1, -1.0, p)
    v2 = jnp.max(p2, axis=-1, keepdims=True)
    i2 = jnp.min(jnp.where(p2 == v2, lane_f, float(LANES)), axis=-1, keepdims=True).astype(I32)
    den = v1 + v2
    w1 = gp * v1 / den
    w2 = gp * v2 / den
    e1 = i1 - MOE_GROUPS
    e2 = i2 - MOE_GROUPS

    hit1 = lane == e1
    hit2 = lane == e2
    onehot = jnp.where(hit1 | hit2, 1.0, 0.0)
    r = lax.broadcasted_iota(I32, (tm, tm), 0)
    c = lax.broadcasted_iota(I32, (tm, tm), 1)
    before = _mm(jnp.where(r > c, 1.0, 0.0), onehot) + cnt_s[...]
    r1 = jnp.sum(jnp.where(hit1, before, 0.0), axis=-1, keepdims=True).astype(I32)
    r2 = jnp.sum(jnp.where(hit2, before, 0.0), axis=-1, keepdims=True).astype(I32)
    cnt_s[...] = cnt_s[...] + jnp.sum(onehot, axis=0, keepdims=True)
    cnt_ref[...] = cnt_s[...].astype(I32)
    ri_ref[...] = jnp.where(lane == 0, e1, jnp.where(lane == 1, e2, jnp.where(lane == 2, r1, jnp.where(lane == 3, r2, 0))))
    rw_ref[...] = jnp.where(lane == 0, w1, jnp.where(lane == 1, w2, 0.0))


def _out_proj_route(o, x, mod, w_bf16, norm2_g, w_router, b_router):
    b, t, d = x.shape
    kdim = o.shape[-1]
    bb, tt = _tiles(b, t)
    tm = bb * tt
    nt = t // tt
    tok = lambda i, j: (i * nt + j, 0)
    modspec = lambda idx: pl.BlockSpec((bb, 1, d), lambda i, j: (i, 0, idx))
    return pl.pallas_call(
        _outproj_kernel,
        grid=(b // bb, nt),
        in_specs=[pl.BlockSpec((bb, tt, kdim), lambda i, j: (i, j, 0)),
                  pl.BlockSpec((bb, tt, d), lambda i, j: (i, j, 0)),
                  modspec(2),
                  pl.BlockSpec((kdim, d), lambda i, j: (0, 0)),
                  pl.BlockSpec((1, d), lambda i, j: (0, 0)),
                  modspec(4), modspec(3),
                  pl.BlockSpec((d, LANES), lambda i, j: (0, 0)),
                  pl.BlockSpec((1, LANES), lambda i, j: (0, 0))],
        out_specs=[pl.BlockSpec((bb, tt, d), lambda i, j: (i, j, 0)),
                   pl.BlockSpec((tm, d), tok),
                   pl.BlockSpec((tm, LANES), tok),
                   pl.BlockSpec((tm, LANES), tok),
                   pl.BlockSpec((1, LANES), lambda i, j: (0, 0))],
        out_shape=[jax.ShapeDtypeStruct((b, t, d), F32),
                   jax.ShapeDtypeStruct((b * t, d), F32),
                   jax.ShapeDtypeStruct((b * t, LANES), I32),
                   jax.ShapeDtypeStruct((b * t, LANES), F32),
                   jax.ShapeDtypeStruct((1, LANES), I32)],
        scratch_shapes=[pltpu.VMEM((1, LANES), F32)],
        compiler_params=_cparams(("arbitrary", "arbitrary")),
        name="out_proj_route",
    )(o, x, mod, w_bf16, norm2_g.reshape(1, d), mod, mod, w_router, b_router)


def _dispatch_kernel(dest_ref, h2_ref, xb_in_ref, xb_ref, sem):
    del xb_in_ref
    tm = h2_ref.shape[0]

    def copy(r, k):
        return pltpu.make_async_copy(h2_ref.at[pl.ds(r, 1), :],
                                     xb_ref.at[pl.ds(dest_ref[0, 0, MOE_TOPK * r + k], 1), :], sem)

    for r in range(tm):
        for k in range(MOE_TOPK):
            copy(r, k).start()
    for r in range(tm):
        for k in range(MOE_TOPK):
            copy(r, k).wait()


def _dispatch(h2, dest_tiles, cap):
    n, d = h2.shape
    nt, _, per = dest_tiles.shape
    tm = per // MOE_TOPK
    return pl.pallas_call(
        _dispatch_kernel,
        grid=(nt,),
        in_specs=[pl.BlockSpec((1, 1, per), lambda i: (i, 0, 0), memory_space=pltpu.SMEM),
                  pl.BlockSpec((tm, d), lambda i: (i, 0)),
                  pl.BlockSpec(memory_space=pl.ANY)],
        out_specs=pl.BlockSpec(memory_space=pl.ANY),
        out_shape=jax.ShapeDtypeStruct((cap, d), F32),
        scratch_shapes=[pltpu.SemaphoreType.DMA],
        input_output_aliases={2: 0},
        compiler_params=_cparams(("arbitrary",)),
        name="moe_dispatch",
    )(dest_tiles, h2, jnp.zeros((cap, d), F32))


def _expert_kernel(be_ref, nu_ref, x_ref, wg_ref, wu_ref, wd_ref, y_ref):
    used = pl.program_id(0) < nu_ref[0]

    @pl.when(used)
    def _():
        x = x_ref[...]
        hid = _silu(_mm(x, wg_ref[0, 0])) * _mm(x, wu_ref[0, 0])
        y_ref[...] = _mm(hid, wd_ref[0, 0])

    @pl.when(jnp.logical_not(used))
    def _():
        y_ref[...] = jnp.zeros_like(y_ref)


def _experts(xb, rows, block_e, n_used, layer, w_gate, w_up, w_down):
    cap, d = xb.shape
    nb = cap // rows
    last = lambda i, nu: jnp.maximum(jnp.minimum(i, nu[0] - 1), 0)
    blk = lambda i, be, nu: (last(i, nu), 0)
    wsel = lambda i, be, nu: (layer, be[last(i, nu)], 0, 0)
    return pl.pallas_call(
        _expert_kernel,
        grid_spec=pltpu.PrefetchScalarGridSpec(
            num_scalar_prefetch=2,
            grid=(nb,),
            in_specs=[pl.BlockSpec((rows, d), blk),
                      pl.BlockSpec((1, 1, d, D_EXPERT), wsel),
                      pl.BlockSpec((1, 1, d, D_EXPERT), wsel),
                      pl.BlockSpec((1, 1, D_EXPERT, d), wsel)],
            out_specs=pl.BlockSpec((rows, d), lambda i, be, nu: (i, 0))),
        out_shape=jax.ShapeDtypeStruct((cap, d), F32),
        compiler_params=_cparams(("arbitrary",)),
        name="moe_experts",
    )(block_e, n_used, xb, w_gate, w_up, w_down)


def _combine_kernel(final, dest_ref, x1_ref, rw_ref, g2_ref, fg_ref, yb_ref, o_ref, buf_s, sem):
    bb, tt, d = x1_ref.shape
    tm = bb * tt

    def copy(r, k):
        return pltpu.make_async_copy(yb_ref.at[pl.ds(dest_ref[0, 0, MOE_TOPK * r + k], 1), :],
                                     buf_s.at[k, pl.ds(r, 1), :], sem)

    for r in range(tm):
        for k in range(MOE_TOPK):
            copy(r, k).start()
    for r in range(tm):
        for k in range(MOE_TOPK):
            copy(r, k).wait()
    rw = rw_ref[...]
    moe = buf_s[0] * rw[:, 0:1] + buf_s[1] * rw[:, 1:2]
    x2 = x1_ref[...] + g2_ref[...] * moe.reshape(bb, tt, d)
    if final:
        x2 = x2 * lax.rsqrt(jnp.mean(x2 * x2, axis=-1, keepdims=True) + EPS) * fg_ref[...]
    o_ref[...] = x2


def _combine(x1, route_w, mod, yb, dest_tiles, final_g, final):
    b, t, d = x1.shape
    bb, tt = _tiles(b, t)
    tm = bb * tt
    nt = t // tt
    per = dest_tiles.shape[-1]
    return pl.pallas_call(
        functools.partial(_combine_kernel, final),
        grid=(b // bb, nt),
        in_specs=[pl.BlockSpec((1, 1, per), lambda i, j: (i * nt + j, 0, 0), memory_space=pltpu.SMEM),
                  pl.BlockSpec((bb, tt, d), lambda i, j: (i, j, 0)),
                  pl.BlockSpec((tm, LANES), lambda i, j: (i * nt + j, 0)),
                  pl.BlockSpec((bb, 1, d), lambda i, j: (i, 0, 5)),
                  pl.BlockSpec((1, d), lambda i, j: (0, 0)),
                  pl.BlockSpec(memory_space=pl.ANY)],
        out_specs=pl.BlockSpec((bb, tt, d), lambda i, j: (i, j, 0)),
        out_shape=jax.ShapeDtypeStruct((b, t, d), F32),
        scratch_shapes=[pltpu.VMEM((MOE_TOPK, tm, d), F32), pltpu.SemaphoreType.DMA],
        compiler_params=_cparams(("arbitrary", "arbitrary")),
        name="moe_combine",
    )(dest_tiles, x1, route_w, mod, final_g.reshape(1, d), yb)


def _moe(layer, x1, h2, route_i, route_w, counts, mod, w_gate, w_up, w_down, final_g, final):
    b, t, d = x1.shape
    n = b * t
    bb, tt = _tiles(b, t)
    tm = bb * tt
    n_asg = n * MOE_TOPK
    rows = min(EXPERT_ROWS_MAX, max(EXPERT_ROWS_MIN, pl.next_power_of_2(n_asg // (2 * N_EXPERTS))))
    nb = (n_asg + N_EXPERTS * (rows - 1) + rows - 1) // rows
    cap = nb * rows
    cnt = counts[0, :N_EXPERTS]
    padded = (cnt + rows - 1) // rows * rows
    ends = jnp.cumsum(padded)
    pstart = ends - padded
    eid = route_i[:, :MOE_TOPK]
    rank = route_i[:, MOE_TOPK:2 * MOE_TOPK]
    first_slot = jnp.sum(jnp.where(eid[..., None] == jnp.arange(N_EXPERTS, dtype=I32), pstart, 0), axis=-1)
    dest_tiles = (first_slot + rank).astype(I32).reshape(n // tm, 1, tm * MOE_TOPK)
    blk_start = jnp.arange(nb, dtype=I32) * rows
    block_e = jnp.minimum(jnp.sum(blk_start[:, None] >= ends[None, :], axis=-1), N_EXPERTS - 1).astype(I32)
    n_used = (ends[-1:] // rows).astype(I32)
    xb = _dispatch(h2, dest_tiles, cap)
    yb = _experts(xb, rows, block_e, n_used, layer, w_gate, w_up, w_down)
    return _combine(x1, route_w, mod, yb, dest_tiles, final_g, final)


def _pad_cols(w, n):
    return jnp.pad(w, ((0, 0), (0, n - w.shape[1])))


def _trunk(x, mods, gdn_conv, gdn_ssm, ssd_conv, ssd_ssm, w):
    b, t, d = x.shape
    gdn_main = GDN_CONV_DIM + GDN_VD
    w_in = jnp.concatenate([w['gdn_w_in'][0][:, :gdn_main], _pad_cols(w['gdn_w_in'][0][:, gdn_main:], LANES)],
                           axis=1).astype(BF16)
    proj = _in_proj(x, w['norm1_g'][0], mods[0], 1, 0, w_in)
    o, gdn_state = _gdn_core(proj, gdn_conv[0], gdn_ssm[0], w['gdn_conv_w'][0], w['gdn_A_log'][0],
                             w['gdn_dt_bias'][0], w['gdn_norm_g'][0])
    gdn_hist = proj[:, t - (CONV_W - 1):, :GDN_CONV_DIM]
    x = _layer_tail(0, o, x, mods[0], w['gdn_w_out'][0], w, False)
    ssd_main = SSD_INNER + SSD_CONV_DIM
    dt_cols = [_pad_cols(w['ssd_w_in'][0][:, ssd_main + g * SSD_HPG: ssd_main + (g + 1) * SSD_HPG], LANES)
               for g in range(SSD_GROUPS)]
    w_in = jnp.concatenate([w['ssd_w_in'][0][:, :ssd_main]] + dt_cols, axis=1).astype(BF16)
    proj = _in_proj(x, w['norm1_g'][1], mods[1], 1, 0, w_in)
    y, ssd_state = _ssd_core(proj, ssd_conv[0], ssd_ssm[0], w['ssd_conv_w'][0], w['ssd_conv_b'][0],
                             w['ssd_dt_bias'][0], w['ssd_A_log'][0], w['ssd_D'][0], w['ssd_norm_g'][0])
    ssd_hist = proj[:, t - (CONV_W - 1):, SSD_INNER:ssd_main]
    y_out = _layer_tail(1, y, x, mods[1], w['ssd_w_out'][0], w, True)
    return y_out, gdn_hist[None], gdn_state[None], ssd_hist[None], ssd_state[None]


def _layer_tail(i, mixed, x, mod, w_out, w, final):
    d = x.shape[-1]
    w_router = _pad_cols(jnp.concatenate([w['moe_w_group'][i], w['moe_w_expert'][i]], axis=1), LANES)
    b_router = _pad_cols(jnp.concatenate([w['moe_b_group'][i], w['moe_b_expert'][i]]).reshape(1, -1), LANES)
    x1, h2, route_i, route_w, counts = _out_proj_route(mixed, x, mod, w_out.astype(BF16), w['norm2_g'][i],
                                                       w_router, b_router)
    del d
    return _moe(i, x1, h2, route_i, route_w, counts, mod, w['moe_w_gate'], w['moe_w_up'], w['moe_w_down'],
                w['final_norm_g'], final)


def kernel(x_prompt, x_sample, state_gdn_conv, state_gdn_ssm, state_ssd_conv, state_ssd_ssm, c_prompt, c_sample,
           norm1_g, norm2_g, ada_w, ada_b, gdn_w_in, gdn_conv_w, gdn_A_log, gdn_dt_bias, gdn_norm_g, gdn_w_out,
           ssd_w_in, ssd_conv_w, ssd_conv_b, ssd_dt_bias, ssd_A_log, ssd_D, ssd_norm_g, ssd_w_out,
           moe_w_group, moe_b_group, moe_w_expert, moe_b_expert, moe_w_gate, moe_w_up, moe_w_down, final_norm_g):
    w = {'norm1_g': norm1_g, 'norm2_g': norm2_g, 'gdn_w_in': gdn_w_in, 'gdn_conv_w': gdn_conv_w,
         'gdn_A_log': gdn_A_log, 'gdn_dt_bias': gdn_dt_bias, 'gdn_norm_g': gdn_norm_g, 'gdn_w_out': gdn_w_out,
         'ssd_w_in': ssd_w_in, 'ssd_conv_w': ssd_conv_w, 'ssd_conv_b': ssd_conv_b, 'ssd_dt_bias': ssd_dt_bias,
         'ssd_A_log': ssd_A_log, 'ssd_D': ssd_D, 'ssd_norm_g': ssd_norm_g, 'ssd_w_out': ssd_w_out,
         'moe_w_group': moe_w_group, 'moe_b_group': moe_b_group, 'moe_w_expert': moe_w_expert,
         'moe_b_expert': moe_b_expert, 'moe_w_gate': moe_w_gate, 'moe_w_up': moe_w_up, 'moe_w_down': moe_w_down,
         'final_norm_g': final_norm_g}
    bp = x_prompt.shape[0]
    bs = x_sample.shape[0]
    dt_ = x_prompt.dtype
    n_seq = bp + bs
    n_pad = -n_seq % SUBLANES
    c_all = jnp.concatenate([c_prompt, c_sample, jnp.zeros((n_pad, c_prompt.shape[1]), dt_)], axis=0)
    mod_all = _ada_mod(c_all, ada_w, ada_b)
    mods_p = [mod_all[l, :bp][:, None, :] for l in range(DEPTH)]
    mods_s = [mod_all[l, bp:n_seq][:, None, :] for l in range(DEPTH)]
    n_gdn = state_gdn_conv.shape[0]
    n_ssd = state_ssd_conv.shape[0]
    z_gc = jnp.zeros((n_gdn, bp) + state_gdn_conv.shape[2:], dt_)
    z_gs = jnp.zeros((n_gdn, bp) + state_gdn_ssm.shape[2:], dt_)
    z_sc = jnp.zeros((n_ssd, bp) + state_ssd_conv.shape[2:], dt_)
    z_ss = jnp.zeros((n_ssd, bp) + state_ssd_ssm.shape[2:], dt_)
    y_p, p_gc, p_gs, p_sc, p_ss = _trunk(x_prompt, mods_p, z_gc, z_gs, z_sc, z_ss, w)
    y_s, s_gc, s_gs, s_sc, s_ss = _trunk(x_sample, mods_s, state_gdn_conv, state_gdn_ssm,
                                         state_ssd_conv, state_ssd_ssm, w)
    return (y_p, y_s, p_gc, p_gs, p_sc, p_ss, s_gc, s_gs, s_sc, s_ss)
```

```python
import functools

import jax
import jax.numpy as jnp
from jax import lax
from jax.experimental import pallas as pl
from jax.experimental.pallas import tpu as pltpu

F32 = jnp.float32
BF16 = jnp.bfloat16
I32 = jnp.int32

D_MODEL = 1024
DEPTH = 2
CHUNK = 64
CONV_W = 4
EPS = 1e-6
GDN_HEADS = 8
GDN_DK = 128
GDN_DV = 128
GDN_QK = GDN_HEADS * GDN_DK
GDN_VD = GDN_HEADS * GDN_DV
GDN_CONV_DIM = 2 * GDN_QK + GDN_VD
SSD_INNER = 2 * D_MODEL
SSD_HEADDIM = 64
SSD_HEADS = SSD_INNER // SSD_HEADDIM
SSD_GROUPS = 4
SSD_HPG = SSD_HEADS // SSD_GROUPS
SSD_DSTATE = 128
SSD_GS = SSD_INNER // SSD_GROUPS
SSD_CONV_DIM = SSD_INNER + 2 * SSD_GROUPS * SSD_DSTATE
MOE_GROUPS = 4
MOE_PER_GROUP = 8
N_EXPERTS = MOE_GROUPS * MOE_PER_GROUP
MOE_TOPK = 2
D_EXPERT = 512

LANES = 128
SUBLANES = 8
VMEM_LIMIT = 48 * 1024 * 1024

ROW_TILE = 256
MOE_MOVE_ROWS = 1024
EXPERT_ROWS_MIN = 128
EXPERT_ROWS_MAX = 512
CONV_PAD = SUBLANES
SSD_GROUPS_PER_STEP = 4


def _cparams(sem):
    return pltpu.CompilerParams(dimension_semantics=sem, vmem_limit_bytes=VMEM_LIMIT)


def _mm(a, b):
    return jnp.dot(a.astype(BF16), b.astype(BF16), preferred_element_type=F32)


def _mm_nt(a, b):
    return lax.dot_general(a.astype(BF16), b.astype(BF16), (((1,), (1,)), ((), ())),
                           preferred_element_type=F32)


def _mm_tn(a, b):
    return lax.dot_general(a.astype(BF16), b.astype(BF16), (((0,), (0,)), ((), ())),
                           preferred_element_type=F32)


def _mm_f32(a, b):
    return jnp.dot(a, b, preferred_element_type=F32, precision=lax.Precision.HIGHEST)


def _silu(x):
    h = 0.5 * x
    return h + h * jnp.tanh(h)


def _softplus(x):
    return jnp.maximum(x, 0.0) + jnp.log1p(jnp.exp(-jnp.abs(x)))


def _tiles(b, t, rows=ROW_TILE):
    rows = min(rows, b * t)
    if t >= rows:
        assert t % rows == 0
        return 1, rows
    assert rows % t == 0 and b % (rows // t) == 0
    return rows // t, t


def _ada_kernel(c_ref, w_ref, b_ref, o_ref):
    o_ref[0] = _mm(_silu(c_ref[...]), w_ref[0]) + b_ref[0]


def _ada_mod(c_all, ada_w, ada_b):
    bp, d = c_all.shape
    n = ada_w.shape[-1]
    tn = 1024
    return pl.pallas_call(
        _ada_kernel,
        grid=(DEPTH, n // tn),
        in_specs=[pl.BlockSpec((bp, d), lambda l, j: (0, 0)),
                  pl.BlockSpec((1, d, tn), lambda l, j: (l, 0, j)),
                  pl.BlockSpec((1, 1, tn), lambda l, j: (l, 0, j))],
        out_specs=pl.BlockSpec((1, bp, tn), lambda l, j: (l, 0, j)),
        out_shape=jax.ShapeDtypeStruct((DEPTH, bp, n), F32),
        compiler_params=_cparams(("arbitrary", "arbitrary")),
        name="ada_mod",
    )(c_all, ada_w, ada_b.reshape(DEPTH, 1, n))


def _inproj_kernel(x_ref, g_ref, sc_ref, sh_ref, w_ref, o_ref):
    bb, tt, d = x_ref.shape
    x = x_ref[...]
    y = x * lax.rsqrt(jnp.mean(x * x, axis=-1, keepdims=True) + EPS) * g_ref[...]
    h = y * (1.0 + sc_ref[...]) + sh_ref[...]
    o = _mm(h.reshape(bb * tt, d), w_ref[...])
    o_ref[...] = o.reshape(bb, tt, o.shape[-1])


def _in_proj(x, norm_g, mod, sc_idx, sh_idx, w_bf16):
    b, t, d = x.shape
    n = w_bf16.shape[1]
    bb, tt = _tiles(b, t)
    return pl.pallas_call(
        _inproj_kernel,
        grid=(b // bb, t // tt),
        in_specs=[pl.BlockSpec((bb, tt, d), lambda i, j: (i, j, 0)),
                  pl.BlockSpec((1, d), lambda i, j: (0, 0)),
                  pl.BlockSpec((bb, 1, d), lambda i, j: (i, 0, sc_idx)),
                  pl.BlockSpec((bb, 1, d), lambda i, j: (i, 0, sh_idx)),
                  pl.BlockSpec((d, n), lambda i, j: (0, 0))],
        out_specs=pl.BlockSpec((bb, tt, n), lambda i, j: (i, j, 0)),
        out_shape=jax.ShapeDtypeStruct((b, t, n), F32),
        compiler_params=_cparams(("arbitrary", "arbitrary")),
        name="in_proj",
    )(x, norm_g.reshape(1, d), mod, mod, w_bf16)


CONV_H0 = CONV_PAD - (CONV_W - 1)


def _causal_conv(xp_ref, w_ref, row0, n_rows, cols):
    e = xp_ref[row0:row0 + CONV_PAD + n_rows, cols]
    acc = e * w_ref[0:1, cols]
    for i in range(1, CONV_W):
        acc = pltpu.roll(acc, 1, 0) + e * w_ref[i:i + 1, cols]
    return acc[CONV_PAD:]


def _chunk_cumsum(v, lc):
    tt = v.shape[0]
    shift = lc.bit_length() - 1
    r = lax.broadcasted_iota(I32, (tt, tt), 0)
    c = lax.broadcasted_iota(I32, (tt, tt), 1)
    same = lax.shift_right_logical(r, shift) == lax.shift_right_logical(c, shift)
    tri = jnp.where(same & (r >= c), 1.0, 0.0).astype(BF16)
    hi = v.astype(BF16)
    r1 = v - hi.astype(F32)
    mid = r1.astype(BF16)
    lo = (r1 - mid.astype(F32)).astype(BF16)
    dot = functools.partial(jnp.dot, preferred_element_type=F32)
    cum = dot(tri, hi) + dot(tri, mid) + dot(tri, lo)
    total = jnp.concatenate([jnp.broadcast_to(cum[e - 1:e, :], (lc, v.shape[1])) for e in range(lc, tt + 1, lc)],
                            axis=0)
    return cum, total


def _split_bf16(a):
    hi = a.astype(BF16)
    return hi, (a - hi.astype(F32)).astype(BF16)


def _mm_split(a, b):
    ah, al = _split_bf16(a)
    bh, bl = _split_bf16(b)
    dot = functools.partial(jnp.dot, preferred_element_type=F32)
    return dot(ah, bh) + dot(ah, bl) + dot(al, bh)


def _gdn_kernel(lc, qp_ref, kp_ref, vp_ref, z_ref, gt_ref, hq_ref, hk_ref, hv_ref,
                wq_ref, wk_ref, wv_ref, s0_ref, alog_ref, dtb_ref, ng_ref,
                o_ref, sout_ref,
                xq_s, xk_s, xv_s, st_s):
    t = pl.program_id(1)
    tt = qp_ref.shape[1]
    first = t == 0
    n_chunks = tt // lc
    heads = range(GDN_HEADS)
    sls = [slice(h * GDN_DK, (h + 1) * GDN_DK) for h in heads]

    @pl.when(first)
    def _():
        st_s[...] = s0_ref[0]
        for xp_s, hist in ((xq_s, hq_ref), (xk_s, hk_ref), (xv_s, hv_ref)):
            xp_s[0:CONV_H0, :] = jnp.zeros((CONV_H0, xp_s.shape[1]), F32)
            xp_s[CONV_H0:CONV_PAD, :] = hist[0]

    gates = gt_ref[0]
    beta = jax.nn.sigmoid(gates)
    g = -jnp.exp(alog_ref[...]) * _softplus(gates + dtb_ref[...])
    gam, gend = _chunk_cumsum(g, lc)
    egam = jnp.exp(gam)
    eend = jnp.exp(gend - gam)
    rw = lax.broadcasted_iota(I32, (lc, LANES), 0)
    cw = lax.broadcasted_iota(I32, (lc, LANES), 1)
    incl = (rw >= cw) & (cw < lc)
    strict = (rw > cw) & (cw < lc)
    right = (cw >= lc) & (cw < 2 * lc)
    eye_right = jnp.where(cw == rw + lc, 1.0, 0.0)

    xq_s[CONV_PAD:CONV_PAD + tt, :] = qp_ref[0]
    xk_s[CONV_PAD:CONV_PAD + tt, :] = kp_ref[0]
    xv_s[CONV_PAD:CONV_PAD + tt, :] = vp_ref[0]

    def conv_head(xp_ref, w_ref, ci, sl):
        return _silu(_causal_conv(xp_ref, w_ref, ci * lc, lc, sl))

    def prepare(ci, h):
        qh = conv_head(xq_s, wq_ref, ci, sls[h])
        kh = conv_head(xk_s, wk_ref, ci, sls[h])
        vh = conv_head(xv_s, wv_ref, ci, sls[h])
        qh = qh * lax.rsqrt(jnp.sum(qh * qh, axis=-1, keepdims=True) + EPS) * (GDN_DK ** -0.5)
        kh = kh * lax.rsqrt(jnp.sum(kh * kh, axis=-1, keepdims=True) + EPS)
        return qh, kh, vh

    def recur(ci, qkv):
        rows = slice(ci * lc, (ci + 1) * lc)
        qs = [qkv[h][0] for h in heads]
        ks = [qkv[h][1] for h in heads]
        vs = [qkv[h][2] for h in heads]
        gam_c = gam[rows, :]
        gam_t = jnp.concatenate([gam_c, jnp.zeros((LANES - lc, LANES), F32)], axis=0).T
        b_cols = [beta[rows, h:h + 1] for h in heads]
        egs = [egam[rows, GDN_HEADS + h:GDN_HEADS + h + 1] for h in heads]
        dec_incl = [jnp.exp(jnp.where(incl, gam_c[:, GDN_HEADS + h:GDN_HEADS + h + 1]
                                      - gam_t[GDN_HEADS + h:GDN_HEADS + h + 1, :], -jnp.inf)) for h in heads]
        zrow = jnp.zeros((LANES - lc, GDN_DK), F32)
        qkk = [_mm_nt(jnp.concatenate([qs[h], ks[h]], axis=0), jnp.concatenate([ks[h], zrow], axis=0))
               for h in heads]
        qk = [qkk[h][:lc] * dec_incl[h] for h in heads]
        a = [b_cols[h] * jnp.where(strict, dec_incl[h], 0.0) * qkk[h][lc:] for h in heads]
        yield
        cs = [eye_right - m for m in a]
        n = 1
        while n < lc:
            nxt_cs = []
            for cm in cs:
                ch, cl = _split_bf16(cm)
                ph, pl_ = ch[:, :lc], cl[:, :lc]
                dot = functools.partial(jnp.dot, preferred_element_type=F32)
                nxt_cs.append(dot(ph, ch) + dot(ph, cl) + dot(pl_, ch) + jnp.where(right, cm, 0.0))
            cs = nxt_cs
            n *= 2
            yield
        pad = [jnp.zeros((n_rows, GDN_DV + GDN_DK), F32) for n_rows in (lc, LANES - 2 * lc) if n_rows]
        rhs = [jnp.concatenate([pad[0], jnp.concatenate([b_cols[h] * vs[h], (b_cols[h] * egs[h]) * ks[h]], axis=-1)]
                               + pad[1:], axis=0) for h in heads]
        x = [_mm(cs[h], rhs[h]) for h in heads]
        yield
        s = [st_s[h] for h in heads]
        both = [_mm(jnp.concatenate([x[h][:, GDN_DV:], qs[h] * egs[h]], axis=0), s[h]) for h in heads]
        w = [x[h][:, :GDN_DV] - both[h][:lc] for h in heads]
        o = [both[h][lc:] + _mm(qk[h][:, :lc], w[h]) for h in heads]
        yield
        cdec = jnp.exp(gend[ci * lc:ci * lc + 1, :])
        for h in heads:
            gl = GDN_HEADS + h
            k_end = ks[h] * eend[rows, gl:gl + 1]
            st_s[h] = cdec[:, gl:gl + 1] * s[h] + _mm_tn(k_end, w[h])
        yield
        for h in heads:
            on = o[h] * lax.rsqrt(jnp.mean(o[h] * o[h], axis=-1, keepdims=True) + EPS) * ng_ref[...]
            o_ref[0, rows, sls[h]] = (on * _silu(z_ref[0, rows, sls[h]])).astype(o_ref.dtype)

    ready = [prepare(0, h) for h in heads]
    for ci in range(n_chunks):
        todo = iter(heads if ci + 1 < n_chunks else ())
        nxt = []
        for _ in recur(ci, ready):
            h = next(todo, None)
            if h is not None:
                nxt.append(prepare(ci + 1, h))
        nxt.extend(prepare(ci + 1, h) for h in todo)
        ready = nxt

    xq_s[CONV_H0:CONV_PAD, :] = xq_s[tt + CONV_H0:tt + CONV_PAD, :]
    xk_s[CONV_H0:CONV_PAD, :] = xk_s[tt + CONV_H0:tt + CONV_PAD, :]
    xv_s[CONV_H0:CONV_PAD, :] = xv_s[tt + CONV_H0:tt + CONV_PAD, :]

    @pl.when(t == pl.num_programs(1) - 1)
    def _():
        sout_ref[0] = st_s[...]


def _gdn_core(proj, conv_hist, s0, conv_w, a_log, dt_bias, norm_g):
    b, t, _ = proj.shape
    lc = min(CHUNK, t)
    tt = min(ROW_TILE, t)
    nq = GDN_QK // GDN_QK
    lane_pad = jnp.zeros((LANES - 2 * GDN_HEADS,), F32)
    alog_row = jnp.concatenate([jnp.zeros((GDN_HEADS,), F32), a_log, lane_pad]).reshape(1, LANES)
    dtb_row = jnp.concatenate([jnp.zeros((GDN_HEADS,), F32), dt_bias, lane_pad]).reshape(1, LANES)
    gate_blk = (GDN_CONV_DIM + GDN_VD) // LANES
    col = lambda j: pl.BlockSpec((1, tt, GDN_QK), lambda i, s: (i, s, j))
    hist = lambda j: pl.BlockSpec((1, CONV_W - 1, GDN_QK), lambda i, s: (i, 0, j))
    cw = lambda j: pl.BlockSpec((CONV_W, GDN_QK), lambda i, s: (0, j))
    row = pl.BlockSpec((1, LANES), lambda i, s: (0, 0))
    state = pl.BlockSpec((1, GDN_HEADS, GDN_DK, GDN_DV), lambda i, s: (i, 0, 0, 0))
    del nq
    return pl.pallas_call(
        functools.partial(_gdn_kernel, lc),
        grid=(b, t // tt),
        in_specs=[col(0), col(1), col(2), col(3),
                  pl.BlockSpec((1, tt, LANES), lambda i, s: (i, s, gate_blk)),
                  hist(0), hist(1), hist(2), cw(0), cw(1), cw(2), state, row, row, row],
        out_specs=[pl.BlockSpec((1, tt, GDN_VD), lambda i, s: (i, s, 0)), state],
        out_shape=[jax.ShapeDtypeStruct((b, t, GDN_VD), BF16),
                   jax.ShapeDtypeStruct((b, GDN_HEADS, GDN_DK, GDN_DV), F32)],
        scratch_shapes=[pltpu.VMEM((CONV_PAD + tt, GDN_QK), F32)] * 3
        + [pltpu.VMEM((GDN_HEADS, GDN_DK, GDN_DV), F32)],
        compiler_params=_cparams(("arbitrary", "arbitrary")),
        name="gdn_core",
    )(proj, proj, proj, proj, proj, conv_hist, conv_hist, conv_hist, conv_w, conv_w, conv_w,
      s0, alog_row, dtb_row, norm_g.reshape(1, GDN_DV))


def _ssd_kernel(lc, z_ref, xp_ref, bp_ref, cp_ref, dt_ref, hx_ref, hb_ref, hc_ref,
                wx_ref, wb_ref, wc_ref, bx_ref, bb_ref, bc_ref, h0_ref,
                alog_ref, dtb_ref, dskip_ref, ng_ref,
                y_ref, hout_ref,
                xx_s, xb_s, xc_s, h_s):
    t = pl.program_id(2)
    tt = xp_ref.shape[1]
    first = t == 0
    p = SSD_HEADDIM
    gs = SSD_GS
    hrows = SSD_HPG * p
    n_grp = alog_ref.shape[0]

    @pl.when(first)
    def _():
        h_s[...] = h0_ref[0].reshape(n_grp * hrows, SSD_DSTATE)
        for xp_s, hist in ((xx_s, hx_ref), (xb_s, hb_ref), (xc_s, hc_ref)):
            xp_s[0:CONV_H0, :] = jnp.zeros((CONV_H0, xp_s.shape[1]), F32)
            xp_s[CONV_H0:CONV_PAD, :] = hist[0]

    xx_s[CONV_PAD:CONV_PAD + tt, :] = xp_ref[0]
    xb_s[CONV_PAD:CONV_PAD + tt, :] = bp_ref[0]
    xc_s[CONV_PAD:CONV_PAD + tt, :] = cp_ref[0]

    def conv(xp_s, w_ref, b_ref, cols):
        return _silu(_causal_conv(xp_s, w_ref, 0, tt, cols) + b_ref[:, cols])

    shift = lc.bit_length() - 1
    r = lax.broadcasted_iota(I32, (tt, tt), 0)
    c = lax.broadcasted_iota(I32, (tt, tt), 1)
    incl = (lax.shift_right_logical(r, shift) == lax.shift_right_logical(c, shift)) & (r >= c)
    low_half = lax.broadcasted_iota(I32, (tt, LANES), 1) < p

    def per_column(m):
        cols = [jnp.broadcast_to(m[:, e:e + 1], (tt, LANES)) for e in range(SSD_HPG)]
        return jnp.concatenate([jnp.where(low_half, cols[2 * i], cols[2 * i + 1])
                                for i in range(SSD_HPG // 2)], axis=-1)

    def group(k):
        wide = slice(k * gs, (k + 1) * gs)
        lanes = slice(k * LANES, (k + 1) * LANES)
        dt = _softplus(dt_ref[0, :, lanes] + dtb_ref[k])
        gam, gend = _chunk_cumsum(dt * (-jnp.exp(alog_ref[k])), lc)
        yield
        bm = conv(xb_s, wb_ref, bb_ref, lanes)
        gam_t = gam.T
        cm = conv(xc_s, wc_ref, bc_ref, lanes)
        yield
        x = conv(xx_s, wx_ref, bx_ref, wide)
        yield
        eg_x = per_column(jnp.exp(gam))
        xdt = x * per_column(dt)
        xe = xdt * per_column(jnp.exp(gend - gam))
        cb = _mm_nt(cm, bm)
        yield
        parts = []
        for pr in range(SSD_HPG // 2):
            cols = slice(pr * LANES, (pr + 1) * LANES)
            lms = [jnp.exp(jnp.where(incl, gam[:, e:e + 1] - gam_t[e:e + 1, :], -jnp.inf))
                   for e in (2 * pr, 2 * pr + 1)]
            ys = [_mm(cb * lm, xdt[:, cols]) for lm in lms]
            parts.append(jnp.where(low_half, ys[0], ys[1]))
            yield
        y_intra = jnp.concatenate(parts, axis=-1)
        hs = slice(k * hrows, (k + 1) * hrows)
        h = h_s[hs, :]
        inter = []
        for ci in range(tt // lc):
            rs = slice(ci * lc, (ci + 1) * lc)
            inter.append(_mm_nt(cm[rs], h))
            st = _mm_tn(xe[rs], bm[rs])
            cdec = jnp.exp(gend[ci * lc:ci * lc + 1, :])
            h = jnp.concatenate([h[e * p:(e + 1) * p] * cdec[:, e:e + 1] + st[e * p:(e + 1) * p]
                                 for e in range(SSD_HPG)], axis=0)
            yield
        h_s[hs, :] = h
        y = y_intra + jnp.concatenate(inter, axis=0) * eg_x + x * dskip_ref[k]
        y = y * _silu(z_ref[0, :, wide])
        y = y * lax.rsqrt(jnp.mean(y * y, axis=-1, keepdims=True) + EPS) * ng_ref[k]
        y_ref[0, :, wide] = y.astype(y_ref.dtype)

    streams = [group(k) for k in range(n_grp)]
    while streams:
        streams = [s for s in streams if next(s, True) is None]

    xx_s[CONV_H0:CONV_PAD, :] = xx_s[tt + CONV_H0:tt + CONV_PAD, :]
    xb_s[CONV_H0:CONV_PAD, :] = xb_s[tt + CONV_H0:tt + CONV_PAD, :]
    xc_s[CONV_H0:CONV_PAD, :] = xc_s[tt + CONV_H0:tt + CONV_PAD, :]

    @pl.when(t == pl.num_programs(2) - 1)
    def _():
        hout_ref[0] = h_s[...].reshape(n_grp * SSD_HPG, p, SSD_DSTATE)


def _ssd_core(proj, conv_hist, h0, conv_w, conv_b, dt_bias, a_log, d_skip, norm_g):
    b, t, _ = proj.shape
    lc = min(CHUNK, t)
    tt = min(ROW_TILE, t)
    g_n = SSD_GROUPS
    gp = SSD_GROUPS_PER_STEP
    gs = SSD_GS
    wide = gp * gs
    narrow = gp * LANES
    xblk = SSD_INNER // wide
    bblk = 2 * SSD_INNER // narrow
    cblk = bblk + g_n // gp
    dblk = cblk + g_n // gp
    hist_b = SSD_INNER // narrow
    hist_c = hist_b + g_n // gp

    def per_group(v):
        return jnp.pad(v.reshape(g_n, 1, SSD_HPG), ((0, 0), (0, 0), (0, LANES - SSD_HPG)))

    dskip_x = jnp.repeat(d_skip, SSD_HEADDIM).reshape(g_n, 1, gs)
    grow = pl.BlockSpec((gp, 1, LANES), lambda i, g, s: (g, 0, 0))
    cwb = conv_b.reshape(1, SSD_CONV_DIM)
    state = pl.BlockSpec((1, gp * SSD_HPG, SSD_HEADDIM, SSD_DSTATE), lambda i, g, s: (i, g, 0, 0))
    in_specs = [
        pl.BlockSpec((1, tt, wide), lambda i, g, s: (i, s, g)),
        pl.BlockSpec((1, tt, wide), lambda i, g, s: (i, s, xblk + g)),
        pl.BlockSpec((1, tt, narrow), lambda i, g, s: (i, s, bblk + g)),
        pl.BlockSpec((1, tt, narrow), lambda i, g, s: (i, s, cblk + g)),
        pl.BlockSpec((1, tt, narrow), lambda i, g, s: (i, s, dblk + g)),
        pl.BlockSpec((1, CONV_W - 1, wide), lambda i, g, s: (i, 0, g)),
        pl.BlockSpec((1, CONV_W - 1, narrow), lambda i, g, s: (i, 0, hist_b + g)),
        pl.BlockSpec((1, CONV_W - 1, narrow), lambda i, g, s: (i, 0, hist_c + g)),
        pl.BlockSpec((CONV_W, wide), lambda i, g, s: (0, g)),
        pl.BlockSpec((CONV_W, narrow), lambda i, g, s: (0, hist_b + g)),
        pl.BlockSpec((CONV_W, narrow), lambda i, g, s: (0, hist_c + g)),
        pl.BlockSpec((1, wide), lambda i, g, s: (0, g)),
        pl.BlockSpec((1, narrow), lambda i, g, s: (0, hist_b + g)),
        pl.BlockSpec((1, narrow), lambda i, g, s: (0, hist_c + g)),
        state,
        grow, grow,
        pl.BlockSpec((gp, 1, gs), lambda i, g, s: (g, 0, 0)),
        pl.BlockSpec((gp, 1, gs), lambda i, g, s: (g, 0, 0)),
    ]
    return pl.pallas_call(
        functools.partial(_ssd_kernel, lc),
        grid=(b, g_n // gp, t // tt),
        in_specs=in_specs,
        out_specs=[pl.BlockSpec((1, tt, wide), lambda i, g, s: (i, s, g)), state],
        out_shape=[jax.ShapeDtypeStruct((b, t, SSD_INNER), BF16),
                   jax.ShapeDtypeStruct((b, SSD_HEADS, SSD_HEADDIM, SSD_DSTATE), F32)],
        scratch_shapes=[pltpu.VMEM((CONV_PAD + tt, wide), F32),
                        pltpu.VMEM((CONV_PAD + tt, narrow), F32),
                        pltpu.VMEM((CONV_PAD + tt, narrow), F32),
                        pltpu.VMEM((gp * SSD_HPG * SSD_HEADDIM, SSD_DSTATE), F32)],
        compiler_params=_cparams(("arbitrary", "arbitrary", "arbitrary")),
        name="ssd_core",
    )(proj, proj, proj, proj, proj, conv_hist, conv_hist, conv_hist, conv_w, conv_w, conv_w,
      cwb, cwb, cwb, h0, per_group(a_log), per_group(dt_bias), dskip_x,
      norm_g.reshape(g_n, 1, gs))


def _outproj_kernel(o_ref, x_ref, g1_ref, w_ref, n2_ref, sc_ref, sh_ref, wr_ref, br_ref,
                    x1_ref, h2_ref, ri_ref, rw_ref, cnt_ref, cnt_s):
    bb, tt, d = x_ref.shape
    tm = bb * tt
    step = pl.program_id(0) * pl.num_programs(1) + pl.program_id(1)

    @pl.when(step == 0)
    def _():
        cnt_s[...] = jnp.zeros_like(cnt_s)

    out = jnp.dot(o_ref[...].reshape(tm, o_ref.shape[-1]), w_ref[...], preferred_element_type=F32)
    x1 = x_ref[...] + g1_ref[...] * out.reshape(bb, tt, d)
    x1_ref[...] = x1
    y = x1 * lax.rsqrt(jnp.mean(x1 * x1, axis=-1, keepdims=True) + EPS) * n2_ref[...]
    h2 = (y * (1.0 + sc_ref[...]) + sh_ref[...]).reshape(tm, d)
    h2_ref[...] = h2

    logits = _mm(h2, wr_ref[...]) + br_ref[...]
    lane = lax.broadcasted_iota(I32, (tm, LANES), 1)
    lane_f = lane.astype(F32)
    neg = -jnp.inf
    gl = jnp.where(lane < MOE_GROUPS, logits, neg)
    ge = jnp.exp(gl - jnp.max(gl, axis=-1, keepdims=True))
    grp_p = ge / jnp.sum(ge, axis=-1, keepdims=True)
    gp = jnp.max(grp_p, axis=-1, keepdims=True)
    gi = jnp.min(jnp.where(grp_p == gp, lane_f, float(LANES)), axis=-1, keepdims=True).astype(I32)
    lo = MOE_GROUPS + gi * MOE_PER_GROUP
    emask = (lane >= lo) & (lane < lo + MOE_PER_GROUP)
    sel = jnp.where(emask, logits, neg)
    se = jnp.exp(sel - jnp.max(sel, axis=-1, keepdims=True))
    p = jnp.where(emask, se / jnp.sum(se, axis=-1, keepdims=True), -1.0)
    v1 = jnp.max(p, axis=-1, keepdims=True)
    i1 = jnp.min(jnp.where(p == v1, lane_f, float(LANES)), axis=-1, keepdims=True).astype(I32)
    p2 = jnp.where(lane == i1, -1.0, p)
    v2 = jnp.max(p2, axis=-1, keepdims=True)
    i2 = jnp.min(jnp.where(p2 == v2, lane_f, float(LANES)), axis=-1, keepdims=True).astype(I32)
    den = v1 + v2
    w1 = gp * v1 / den
    w2 = gp * v2 / den
    e1 = i1 - MOE_GROUPS
    e2 = i2 - MOE_GROUPS

    hit1 = lane == e1
    hit2 = lane == e2
    onehot = jnp.where(hit1 | hit2, 1.0, 0.0)
    r = lax.broadcasted_iota(I32, (tm, tm), 0)
    c = lax.broadcasted_iota(I32, (tm, tm), 1)
    before = _mm(jnp.where(r > c, 1.0, 0.0), onehot) + cnt_s[...]
    r1 = jnp.sum(jnp.where(hit1, before, 0.0), axis=-1, keepdims=True).astype(I32)
    r2 = jnp.sum(jnp.where(hit2, before, 0.0), axis=-1, keepdims=True).astype(I32)
    cnt_s[...] = cnt_s[...] + jnp.sum(onehot, axis=0, keepdims=True)
    cnt_ref[...] = cnt_s[...].astype(I32)
    ri_ref[...] = jnp.where(lane == 0, e1, jnp.where(lane == 1, e2, jnp.where(lane == 2, r1, jnp.where(lane == 3, r2, 0))))
    rw_ref[...] = jnp.where(lane == 0, w1, jnp.where(lane == 1, w2, 0.0))


def _out_proj_route(o, x, mod, w_bf16, norm2_g, w_router, b_router):
    b, t, d = x.shape
    kdim = o.shape[-1]
    bb, tt = _tiles(b, t)
    tm = bb * tt
    nt = t // tt
    tok = lambda i, j: (i * nt + j, 0)
    modspec = lambda idx: pl.BlockSpec((bb, 1, d), lambda i, j: (i, 0, idx))
    return pl.pallas_call(
        _outproj_kernel,
        grid=(b // bb, nt),
        in_specs=[pl.BlockSpec((bb, tt, kdim), lambda i, j: (i, j, 0)),
                  pl.BlockSpec((bb, tt, d), lambda i, j: (i, j, 0)),
                  modspec(2),
                  pl.BlockSpec((kdim, d), lambda i, j: (0, 0)),
                  pl.BlockSpec((1, d), lambda i, j: (0, 0)),
                  modspec(4), modspec(3),
                  pl.BlockSpec((d, LANES), lambda i, j: (0, 0)),
                  pl.BlockSpec((1, LANES), lambda i, j: (0, 0))],
        out_specs=[pl.BlockSpec((bb, tt, d), lambda i, j: (i, j, 0)),
                   pl.BlockSpec((tm, d), tok),
                   pl.BlockSpec((tm, LANES), tok),
                   pl.BlockSpec((tm, LANES), tok),
                   pl.BlockSpec((1, LANES), lambda i, j: (0, 0))],
        out_shape=[jax.ShapeDtypeStruct((b, t, d), F32),
                   jax.ShapeDtypeStruct((b * t, d), F32),
                   jax.ShapeDtypeStruct((b * t, LANES), I32),
                   jax.ShapeDtypeStruct((b * t, LANES), F32),
                   jax.ShapeDtypeStruct((1, LANES), I32)],
        scratch_shapes=[pltpu.VMEM((1, LANES), F32)],
        compiler_params=_cparams(("arbitrary", "arbitrary")),
        name="out_proj_route",
    )(o, x, mod, w_bf16, norm2_g.reshape(1, d), mod, mod, w_router, b_router)


def _dispatch_kernel(dest_ref, h2_ref, xb_in_ref, xb_ref, sem):
    del xb_in_ref
    tm = h2_ref.shape[0]

    def copy(r, k):
        return pltpu.make_async_copy(h2_ref.at[pl.ds(r, 1), :],
                                     xb_ref.at[pl.ds(dest_ref[0, 0, MOE_TOPK * r + k], 1), :], sem)

    for r in range(tm):
        for k in range(MOE_TOPK):
            copy(r, k).start()
    for r in range(tm):
        for k in range(MOE_TOPK):
            copy(r, k).wait()


def _dispatch(h2, dest_tiles, cap):
    n, d = h2.shape
    nt, _, per = dest_tiles.shape
    tm = per // MOE_TOPK
    return pl.pallas_call(
        _dispatch_kernel,
        grid=(nt,),
        in_specs=[pl.BlockSpec((1, 1, per), lambda i: (i, 0, 0), memory_space=pltpu.SMEM),
                  pl.BlockSpec((tm, d), lambda i: (i, 0)),
                  pl.BlockSpec(memory_space=pl.ANY)],
        out_specs=pl.BlockSpec(memory_space=pl.ANY),
        out_shape=jax.ShapeDtypeStruct((cap, d), F32),
        scratch_shapes=[pltpu.SemaphoreType.DMA],
        input_output_aliases={2: 0},
        compiler_params=_cparams(("arbitrary",)),
        name="moe_dispatch",
    )(dest_tiles, h2, jnp.zeros((cap, d), F32))


def _expert_kernel(be_ref, nu_ref, x_ref, wg_ref, wu_ref, wd_ref, y_ref):
    used = pl.program_id(0) < nu_ref[0]

    @pl.when(used)
    def _():
        x = x_ref[...]
        hid = _silu(_mm(x, wg_ref[0, 0])) * _mm(x, wu_ref[0, 0])
        y_ref[...] = _mm(hid, wd_ref[0, 0])

    @pl.when(jnp.logical_not(used))
    def _():
        y_ref[...] = jnp.zeros_like(y_ref)


def _experts(xb, rows, block_e, n_used, layer, w_gate, w_up, w_down):
    cap, d = xb.shape
    nb = cap // rows
    last = lambda i, nu: jnp.maximum(jnp.minimum(i, nu[0] - 1), 0)
    blk = lambda i, be, nu: (last(i, nu), 0)
    wsel = lambda i, be, nu: (layer, be[last(i, nu)], 0, 0)
    return pl.pallas_call(
        _expert_kernel,
        grid_spec=pltpu.PrefetchScalarGridSpec(
            num_scalar_prefetch=2,
            grid=(nb,),
            in_specs=[pl.BlockSpec((rows, d), blk),
                      pl.BlockSpec((1, 1, d, D_EXPERT), wsel),
                      pl.BlockSpec((1, 1, d, D_EXPERT), wsel),
                      pl.BlockSpec((1, 1, D_EXPERT, d), wsel)],
            out_specs=pl.BlockSpec((rows, d), lambda i, be, nu: (i, 0))),
        out_shape=jax.ShapeDtypeStruct((cap, d), F32),
        compiler_params=_cparams(("arbitrary",)),
        name="moe_experts",
    )(block_e, n_used, xb, w_gate, w_up, w_down)


def _combine_kernel(final, dest_ref, x1_ref, rw_ref, g2_ref, fg_ref, yb_ref, o_ref, buf_s, sem):
    bb, tt, d = x1_ref.shape
    tm = bb * tt

    def copy(r, k):
        return pltpu.make_async_copy(yb_ref.at[pl.ds(dest_ref[0, 0, MOE_TOPK * r + k], 1), :],
                                     buf_s.at[k, pl.ds(r, 1), :], sem)

    for r in range(tm):
        for k in range(MOE_TOPK):
            copy(r, k).start()
    for r in range(tm):
        for k in range(MOE_TOPK):
            copy(r, k).wait()
    rw = rw_ref[...]
    moe = buf_s[0] * rw[:, 0:1] + buf_s[1] * rw[:, 1:2]
    x2 = x1_ref[...] + g2_ref[...] * moe.reshape(bb, tt, d)
    if final:
        x2 = x2 * lax.rsqrt(jnp.mean(x2 * x2, axis=-1, keepdims=True) + EPS) * fg_ref[...]
    o_ref[...] = x2


def _combine(x1, route_w, mod, yb, dest_tiles, final_g, final):
    b, t, d = x1.shape
    bb, tt = _tiles(b, t, MOE_MOVE_ROWS)
    tm = bb * tt
    nt = t // tt
    per = dest_tiles.shape[-1]
    return pl.pallas_call(
        functools.partial(_combine_kernel, final),
        grid=(b // bb, nt),
        in_specs=[pl.BlockSpec((1, 1, per), lambda i, j: (i * nt + j, 0, 0), memory_space=pltpu.SMEM),
                  pl.BlockSpec((bb, tt, d), lambda i, j: (i, j, 0)),
                  pl.BlockSpec((tm, LANES), lambda i, j: (i * nt + j, 0)),
                  pl.BlockSpec((bb, 1, d), lambda i, j: (i, 0, 5)),
                  pl.BlockSpec((1, d), lambda i, j: (0, 0)),
                  pl.BlockSpec(memory_space=pl.ANY)],
        out_specs=pl.BlockSpec((bb, tt, d), lambda i, j: (i, j, 0)),
        out_shape=jax.ShapeDtypeStruct((b, t, d), F32),
        scratch_shapes=[pltpu.VMEM((MOE_TOPK, tm, d), F32), pltpu.SemaphoreType.DMA],
        compiler_params=_cparams(("arbitrary", "arbitrary")),
        name="moe_combine",
    )(dest_tiles, x1, route_w, mod, final_g.reshape(1, d), yb)


def _moe(layer, x1, h2, route_i, route_w, counts, mod, w_gate, w_up, w_down, final_g, final):
    b, t, d = x1.shape
    n = b * t
    bb, tt = _tiles(b, t, MOE_MOVE_ROWS)
    tm = bb * tt
    n_asg = n * MOE_TOPK
    rows = min(EXPERT_ROWS_MAX, max(EXPERT_ROWS_MIN, pl.next_power_of_2(n_asg // (2 * N_EXPERTS))))
    nb = (n_asg + N_EXPERTS * (rows - 1) + rows - 1) // rows
    cap = nb * rows
    cnt = counts[0, :N_EXPERTS]
    padded = (cnt + rows - 1) // rows * rows
    ends = jnp.cumsum(padded)
    pstart = ends - padded
    eid = route_i[:, :MOE_TOPK]
    rank = route_i[:, MOE_TOPK:2 * MOE_TOPK]
    first_slot = jnp.sum(jnp.where(eid[..., None] == jnp.arange(N_EXPERTS, dtype=I32), pstart, 0), axis=-1)
    dest_tiles = (first_slot + rank).astype(I32).reshape(n // tm, 1, tm * MOE_TOPK)
    blk_start = jnp.arange(nb, dtype=I32) * rows
    block_e = jnp.minimum(jnp.sum(blk_start[:, None] >= ends[None, :], axis=-1), N_EXPERTS - 1).astype(I32)
    n_used = (ends[-1:] // rows).astype(I32)
    xb = _dispatch(h2, dest_tiles, cap)
    yb = _experts(xb, rows, block_e, n_used, layer, w_gate, w_up, w_down)
    return _combine(x1, route_w, mod, yb, dest_tiles, final_g, final)


def _pad_cols(w, n):
    return jnp.pad(w, ((0, 0), (0, n - w.shape[1])))


def _trunk(x, mods, gdn_conv, gdn_ssm, ssd_conv, ssd_ssm, w):
    b, t, d = x.shape
    gdn_main = GDN_CONV_DIM + GDN_VD
    w_in = jnp.concatenate([w['gdn_w_in'][0][:, :gdn_main], _pad_cols(w['gdn_w_in'][0][:, gdn_main:], LANES)],
                           axis=1).astype(BF16)
    proj = _in_proj(x, w['norm1_g'][0], mods[0], 1, 0, w_in)
    o, gdn_state = _gdn_core(proj, gdn_conv[0], gdn_ssm[0], w['gdn_conv_w'][0], w['gdn_A_log'][0],
                             w['gdn_dt_bias'][0], w['gdn_norm_g'][0])
    gdn_hist = proj[:, t - (CONV_W - 1):, :GDN_CONV_DIM]
    x = _layer_tail(0, o, x, mods[0], w['gdn_w_out'][0], w, False)
    ssd_main = SSD_INNER + SSD_CONV_DIM
    dt_cols = [_pad_cols(w['ssd_w_in'][0][:, ssd_main + g * SSD_HPG: ssd_main + (g + 1) * SSD_HPG], LANES)
               for g in range(SSD_GROUPS)]
    w_in = jnp.concatenate([w['ssd_w_in'][0][:, :ssd_main]] + dt_cols, axis=1).astype(BF16)
    proj = _in_proj(x, w['norm1_g'][1], mods[1], 1, 0, w_in)
    y, ssd_state = _ssd_core(proj, ssd_conv[0], ssd_ssm[0], w['ssd_conv_w'][0], w['ssd_conv_b'][0],
                             w['ssd_dt_bias'][0], w['ssd_A_log'][0], w['ssd_D'][0], w['ssd_norm_g'][0])
    ssd_hist = proj[:, t - (CONV_W - 1):, SSD_INNER:ssd_main]
    y_out = _layer_tail(1, y, x, mods[1], w['ssd_w_out'][0], w, True)
    return y_out, gdn_hist[None], gdn_state[None], ssd_hist[None], ssd_state[None]


def _layer_tail(i, mixed, x, mod, w_out, w, final):
    d = x.shape[-1]
    w_router = _pad_cols(jnp.concatenate([w['moe_w_group'][i], w['moe_w_expert'][i]], axis=1), LANES)
    b_router = _pad_cols(jnp.concatenate([w['moe_b_group'][i], w['moe_b_expert'][i]]).reshape(1, -1), LANES)
    x1, h2, route_i, route_w, counts = _out_proj_route(mixed, x, mod, w_out.astype(BF16), w['norm2_g'][i],
                                                       w_router, b_router)
    del d
    return _moe(i, x1, h2, route_i, route_w, counts, mod, w['moe_w_gate'], w['moe_w_up'], w['moe_w_down'],
                w['final_norm_g'], final)


def kernel(x_prompt, x_sample, state_gdn_conv, state_gdn_ssm, state_ssd_conv, state_ssd_ssm, c_prompt, c_sample,
           norm1_g, norm2_g, ada_w, ada_b, gdn_w_in, gdn_conv_w, gdn_A_log, gdn_dt_bias, gdn_norm_g, gdn_w_out,
           ssd_w_in, ssd_conv_w, ssd_conv_b, ssd_dt_bias, ssd_A_log, ssd_D, ssd_norm_g, ssd_w_out,
           moe_w_group, moe_b_group, moe_w_expert, moe_b_expert, moe_w_gate, moe_w_up, moe_w_down, final_norm_g):
    w = {'norm1_g': norm1_g, 'norm2_g': norm2_g, 'gdn_w_in': gdn_w_in, 'gdn_conv_w': gdn_conv_w,
         'gdn_A_log': gdn_A_log, 'gdn_dt_bias': gdn_dt_bias, 'gdn_norm_g': gdn_norm_g, 'gdn_w_out': gdn_w_out,
         'ssd_w_in': ssd_w_in, 'ssd_conv_w': ssd_conv_w, 'ssd_conv_b': ssd_conv_b, 'ssd_dt_bias': ssd_dt_bias,
         'ssd_A_log': ssd_A_log, 'ssd_D': ssd_D, 'ssd_norm_g': ssd_norm_g, 'ssd_w_out': ssd_w_out,
         'moe_w_group': moe_w_group, 'moe_b_group': moe_b_group, 'moe_w_expert': moe_w_expert,
         'moe_b_expert': moe_b_expert, 'moe_w_gate': moe_w_gate, 'moe_w_up': moe_w_up, 'moe_w_down': moe_w_down,
         'final_norm_g': final_norm_g}
    bp = x_prompt.shape[0]
    bs = x_sample.shape[0]
    dt_ = x_prompt.dtype
    n_seq = bp + bs
    n_pad = -n_seq % SUBLANES
    c_all = jnp.concatenate([c_prompt, c_sample, jnp.zeros((n_pad, c_prompt.shape[1]), dt_)], axis=0)
    mod_all = _ada_mod(c_all, ada_w, ada_b)
    mods_p = [mod_all[l, :bp][:, None, :] for l in range(DEPTH)]
    mods_s = [mod_all[l, bp:n_seq][:, None, :] for l in range(DEPTH)]
    n_gdn = state_gdn_conv.shape[0]
    n_ssd = state_ssd_conv.shape[0]
    z_gc = jnp.zeros((n_gdn, bp) + state_gdn_conv.shape[2:], dt_)
    z_gs = jnp.zeros((n_gdn, bp) + state_gdn_ssm.shape[2:], dt_)
    z_sc = jnp.zeros((n_ssd, bp) + state_ssd_conv.shape[2:], dt_)
    z_ss = jnp.zeros((n_ssd, bp) + state_ssd_ssm.shape[2:], dt_)
    y_p, p_gc, p_gs, p_sc, p_ss = _trunk(x_prompt, mods_p, z_gc, z_gs, z_sc, z_ss, w)
    y_s, s_gc, s_gs, s_sc, s_ss = _trunk(x_sample, mods_s, state_gdn_conv, state_gdn_ssm,
                                         state_ssd_conv, state_ssd_ssm, w)
    return (y_p, y_s, p_gc, p_gs, p_sc, p_ss, s_gc, s_gs, s_sc, s_ss)
```

```python
import functools

import jax
import jax.numpy as jnp
from jax import lax
from jax.experimental import pallas as pl
from jax.experimental.pallas import tpu as pltpu

F32 = jnp.float32
BF16 = jnp.bfloat16
I32 = jnp.int32

D_MODEL = 1024
DEPTH = 2
CHUNK = 64
CONV_W = 4
EPS = 1e-6
GDN_HEADS = 8
GDN_DK = 128
GDN_DV = 128
GDN_QK = GDN_HEADS * GDN_DK
GDN_VD = GDN_HEADS * GDN_DV
GDN_CONV_DIM = 2 * GDN_QK + GDN_VD
SSD_INNER = 2 * D_MODEL
SSD_HEADDIM = 64
SSD_HEADS = SSD_INNER // SSD_HEADDIM
SSD_GROUPS = 4
SSD_HPG = SSD_HEADS // SSD_GROUPS
SSD_DSTATE = 128
SSD_GS = SSD_INNER // SSD_GROUPS
SSD_CONV_DIM = SSD_INNER + 2 * SSD_GROUPS * SSD_DSTATE
MOE_GROUPS = 4
MOE_PER_GROUP = 8
N_EXPERTS = MOE_GROUPS * MOE_PER_GROUP
MOE_TOPK = 2
D_EXPERT = 512

LANES = 128
SUBLANES = 8
VMEM_LIMIT = 48 * 1024 * 1024

ROW_TILE = 256
MOE_MOVE_ROWS = 1024
EXPERT_ROWS_MIN = 128
EXPERT_ROWS_MAX = 512
CONV_PAD = SUBLANES
SSD_GROUPS_PER_STEP = 4


def _cparams(sem):
    return pltpu.CompilerParams(dimension_semantics=sem, vmem_limit_bytes=VMEM_LIMIT)


def _mm(a, b):
    return jnp.dot(a.astype(BF16), b.astype(BF16), preferred_element_type=F32)


def _mm_nt(a, b):
    return lax.dot_general(a.astype(BF16), b.astype(BF16), (((1,), (1,)), ((), ())),
                           preferred_element_type=F32)


def _mm_tn(a, b):
    return lax.dot_general(a.astype(BF16), b.astype(BF16), (((0,), (0,)), ((), ())),
                           preferred_element_type=F32)


def _mm_f32(a, b):
    return jnp.dot(a, b, preferred_element_type=F32, precision=lax.Precision.HIGHEST)


def _silu(x):
    h = 0.5 * x
    return h + h * jnp.tanh(h)


def _softplus(x):
    return jnp.maximum(x, 0.0) + jnp.log1p(jnp.exp(-jnp.abs(x)))


def _tiles(b, t, rows=ROW_TILE):
    rows = min(rows, b * t)
    if t >= rows:
        assert t % rows == 0
        return 1, rows
    assert rows % t == 0 and b % (rows // t) == 0
    return rows // t, t


def _ada_kernel(c_ref, w_ref, b_ref, o_ref):
    o_ref[0] = _mm(_silu(c_ref[...]), w_ref[0]) + b_ref[0]


def _ada_mod(c_all, ada_w, ada_b):
    bp, d = c_all.shape
    n = ada_w.shape[-1]
    tn = 1024
    return pl.pallas_call(
        _ada_kernel,
        grid=(DEPTH, n // tn),
        in_specs=[pl.BlockSpec((bp, d), lambda l, j: (0, 0)),
                  pl.BlockSpec((1, d, tn), lambda l, j: (l, 0, j)),
                  pl.BlockSpec((1, 1, tn), lambda l, j: (l, 0, j))],
        out_specs=pl.BlockSpec((1, bp, tn), lambda l, j: (l, 0, j)),
        out_shape=jax.ShapeDtypeStruct((DEPTH, bp, n), F32),
        compiler_params=_cparams(("arbitrary", "arbitrary")),
        name="ada_mod",
    )(c_all, ada_w, ada_b.reshape(DEPTH, 1, n))


def _inproj_kernel(x_ref, g_ref, sc_ref, sh_ref, w_ref, o_ref):
    bb, tt, d = x_ref.shape
    x = x_ref[...]
    y = x * lax.rsqrt(jnp.mean(x * x, axis=-1, keepdims=True) + EPS) * g_ref[...]
    h = y * (1.0 + sc_ref[...]) + sh_ref[...]
    o = _mm(h.reshape(bb * tt, d), w_ref[...])
    o_ref[...] = o.reshape(bb, tt, o.shape[-1])


def _in_proj(x, norm_g, mod, sc_idx, sh_idx, w_bf16):
    b, t, d = x.shape
    n = w_bf16.shape[1]
    bb, tt = _tiles(b, t)
    return pl.pallas_call(
        _inproj_kernel,
        grid=(b // bb, t // tt),
        in_specs=[pl.BlockSpec((bb, tt, d), lambda i, j: (i, j, 0)),
                  pl.BlockSpec((1, d), lambda i, j: (0, 0)),
                  pl.BlockSpec((bb, 1, d), lambda i, j: (i, 0, sc_idx)),
                  pl.BlockSpec((bb, 1, d), lambda i, j: (i, 0, sh_idx)),
                  pl.BlockSpec((d, n), lambda i, j: (0, 0))],
        out_specs=pl.BlockSpec((bb, tt, n), lambda i, j: (i, j, 0)),
        out_shape=jax.ShapeDtypeStruct((b, t, n), F32),
        compiler_params=_cparams(("arbitrary", "arbitrary")),
        name="in_proj",
    )(x, norm_g.reshape(1, d), mod, mod, w_bf16)


CONV_H0 = CONV_PAD - (CONV_W - 1)


def _causal_conv(xp_ref, w_ref, row0, n_rows, cols):
    e = xp_ref[row0:row0 + CONV_PAD + n_rows, cols]
    acc = e * w_ref[0:1, cols]
    for i in range(1, CONV_W):
        acc = pltpu.roll(acc, 1, 0) + e * w_ref[i:i + 1, cols]
    return acc[CONV_PAD:]


def _chunk_cumsum(v, lc):
    tt = v.shape[0]
    shift = lc.bit_length() - 1
    r = lax.broadcasted_iota(I32, (tt, tt), 0)
    c = lax.broadcasted_iota(I32, (tt, tt), 1)
    same = lax.shift_right_logical(r, shift) == lax.shift_right_logical(c, shift)
    tri = jnp.where(same & (r >= c), 1.0, 0.0).astype(BF16)
    hi = v.astype(BF16)
    r1 = v - hi.astype(F32)
    mid = r1.astype(BF16)
    lo = (r1 - mid.astype(F32)).astype(BF16)
    dot = functools.partial(jnp.dot, preferred_element_type=F32)
    cum = dot(tri, hi) + dot(tri, mid) + dot(tri, lo)
    total = jnp.concatenate([jnp.broadcast_to(cum[e - 1:e, :], (lc, v.shape[1])) for e in range(lc, tt + 1, lc)],
                            axis=0)
    return cum, total


def _split_bf16(a):
    hi = a.astype(BF16)
    return hi, (a - hi.astype(F32)).astype(BF16)


def _mm_split(a, b):
    ah, al = _split_bf16(a)
    bh, bl = _split_bf16(b)
    dot = functools.partial(jnp.dot, preferred_element_type=F32)
    return dot(ah, bh) + dot(ah, bl) + dot(al, bh)


def _gdn_kernel(lc, qp_ref, kp_ref, vp_ref, z_ref, gt_ref, hq_ref, hk_ref, hv_ref,
                wq_ref, wk_ref, wv_ref, s0_ref, alog_ref, dtb_ref, ng_ref,
                o_ref, sout_ref,
                xq_s, xk_s, xv_s, st_s):
    t = pl.program_id(1)
    tt = qp_ref.shape[1]
    first = t == 0
    n_chunks = tt // lc
    heads = range(GDN_HEADS)
    sls = [slice(h * GDN_DK, (h + 1) * GDN_DK) for h in heads]

    @pl.when(first)
    def _():
        st_s[...] = s0_ref[0]
        for xp_s, hist in ((xq_s, hq_ref), (xk_s, hk_ref), (xv_s, hv_ref)):
            xp_s[0:CONV_H0, :] = jnp.zeros((CONV_H0, xp_s.shape[1]), F32)
            xp_s[CONV_H0:CONV_PAD, :] = hist[0]

    gates = gt_ref[0]
    beta = jax.nn.sigmoid(gates)
    g = -jnp.exp(alog_ref[...]) * _softplus(gates + dtb_ref[...])
    gam, gend = _chunk_cumsum(g, lc)
    egam = jnp.exp(gam)
    eend = jnp.exp(gend - gam)
    rw = lax.broadcasted_iota(I32, (lc, LANES), 0)
    cw = lax.broadcasted_iota(I32, (lc, LANES), 1)
    incl = (rw >= cw) & (cw < lc)
    strict = (rw > cw) & (cw < lc)
    right = (cw >= lc) & (cw < 2 * lc)
    eye_right = jnp.where(cw == rw + lc, 1.0, 0.0)

    xq_s[CONV_PAD:CONV_PAD + tt, :] = qp_ref[0]
    xk_s[CONV_PAD:CONV_PAD + tt, :] = kp_ref[0]
    xv_s[CONV_PAD:CONV_PAD + tt, :] = vp_ref[0]

    def conv_head(xp_ref, w_ref, ci, sl):
        return _silu(_causal_conv(xp_ref, w_ref, ci * lc, lc, sl))

    def prepare(ci, h):
        qh = conv_head(xq_s, wq_ref, ci, sls[h])
        kh = conv_head(xk_s, wk_ref, ci, sls[h])
        vh = conv_head(xv_s, wv_ref, ci, sls[h])
        qh = qh * (lax.rsqrt(jnp.sum(qh * qh, axis=-1, keepdims=True) + EPS) * (GDN_DK ** -0.5))
        kh = kh * lax.rsqrt(jnp.sum(kh * kh, axis=-1, keepdims=True) + EPS)
        return qh, kh, vh

    def recur(ci, qkv):
        rows = slice(ci * lc, (ci + 1) * lc)
        qs = [qkv[h][0] for h in heads]
        ks = [qkv[h][1] for h in heads]
        vs = [qkv[h][2] for h in heads]
        gam_c = gam[rows, :]
        gam_t = jnp.concatenate([gam_c, jnp.zeros((LANES - lc, LANES), F32)], axis=0).T
        b_cols = [beta[rows, h:h + 1] for h in heads]
        egs = [egam[rows, GDN_HEADS + h:GDN_HEADS + h + 1] for h in heads]
        dec_incl = [jnp.exp(jnp.where(incl, gam_c[:, GDN_HEADS + h:GDN_HEADS + h + 1]
                                      - gam_t[GDN_HEADS + h:GDN_HEADS + h + 1, :], -jnp.inf)) for h in heads]
        zrow = jnp.zeros((LANES - lc, GDN_DK), F32)
        qkk = [_mm_nt(jnp.concatenate([qs[h], ks[h]], axis=0), jnp.concatenate([ks[h], zrow], axis=0))
               for h in heads]
        qk = [qkk[h][:lc] * dec_incl[h] for h in heads]
        a = [b_cols[h] * jnp.where(strict, dec_incl[h], 0.0) * qkk[h][lc:] for h in heads]
        yield
        cs = [eye_right - m for m in a]
        n = 1
        while n < lc:
            nxt_cs = []
            for cm in cs:
                ch, cl = _split_bf16(cm)
                ph, pl_ = ch[:, :lc], cl[:, :lc]
                dot = functools.partial(jnp.dot, preferred_element_type=F32)
                nxt_cs.append(dot(ph, ch) + dot(ph, cl) + dot(pl_, ch) + jnp.where(right, cm, 0.0))
            cs = nxt_cs
            n *= 2
            yield
        pad = [jnp.zeros((n_rows, GDN_DV + GDN_DK), F32) for n_rows in (lc, LANES - 2 * lc) if n_rows]
        rhs = [jnp.concatenate([pad[0], jnp.concatenate([b_cols[h] * vs[h], (b_cols[h] * egs[h]) * ks[h]], axis=-1)]
                               + pad[1:], axis=0) for h in heads]
        x = [_mm(cs[h], rhs[h]) for h in heads]
        yield
        s = [st_s[h] for h in heads]
        both = [_mm(jnp.concatenate([x[h][:, GDN_DV:], qs[h] * egs[h]], axis=0), s[h]) for h in heads]
        w = [x[h][:, :GDN_DV] - both[h][:lc] for h in heads]
        o = [both[h][lc:] + _mm(qk[h][:, :lc], w[h]) for h in heads]
        yield
        cdec = jnp.exp(gend[ci * lc:ci * lc + 1, :])
        for h in heads:
            gl = GDN_HEADS + h
            k_end = ks[h] * eend[rows, gl:gl + 1]
            st_s[h] = cdec[:, gl:gl + 1] * s[h] + _mm_tn(k_end, w[h])
        yield
        for h in heads:
            on = o[h] * lax.rsqrt(jnp.mean(o[h] * o[h], axis=-1, keepdims=True) + EPS) * ng_ref[...]
            o_ref[0, rows, sls[h]] = (on * _silu(z_ref[0, rows, sls[h]])).astype(o_ref.dtype)

    ready = [prepare(0, h) for h in heads]
    for ci in range(n_chunks):
        todo = iter(heads if ci + 1 < n_chunks else ())
        nxt = []
        for _ in recur(ci, ready):
            h = next(todo, None)
            if h is not None:
                nxt.append(prepare(ci + 1, h))
        nxt.extend(prepare(ci + 1, h) for h in todo)
        ready = nxt

    xq_s[CONV_H0:CONV_PAD, :] = xq_s[tt + CONV_H0:tt + CONV_PAD, :]
    xk_s[CONV_H0:CONV_PAD, :] = xk_s[tt + CONV_H0:tt + CONV_PAD, :]
    xv_s[CONV_H0:CONV_PAD, :] = xv_s[tt + CONV_H0:tt + CONV_PAD, :]

    @pl.when(t == pl.num_programs(1) - 1)
    def _():
        sout_ref[0] = st_s[...]


def _gdn_core(proj, conv_hist, s0, conv_w, a_log, dt_bias, norm_g):
    b, t, _ = proj.shape
    lc = min(CHUNK, t)
    tt = min(ROW_TILE, t)
    nq = GDN_QK // GDN_QK
    lane_pad = jnp.zeros((LANES - 2 * GDN_HEADS,), F32)
    alog_row = jnp.concatenate([jnp.zeros((GDN_HEADS,), F32), a_log, lane_pad]).reshape(1, LANES)
    dtb_row = jnp.concatenate([jnp.zeros((GDN_HEADS,), F32), dt_bias, lane_pad]).reshape(1, LANES)
    gate_blk = (GDN_CONV_DIM + GDN_VD) // LANES
    col = lambda j: pl.BlockSpec((1, tt, GDN_QK), lambda i, s: (i, s, j))
    hist = lambda j: pl.BlockSpec((1, CONV_W - 1, GDN_QK), lambda i, s: (i, 0, j))
    cw = lambda j: pl.BlockSpec((CONV_W, GDN_QK), lambda i, s: (0, j))
    row = pl.BlockSpec((1, LANES), lambda i, s: (0, 0))
    state = pl.BlockSpec((1, GDN_HEADS, GDN_DK, GDN_DV), lambda i, s: (i, 0, 0, 0))
    del nq
    return pl.pallas_call(
        functools.partial(_gdn_kernel, lc),
        grid=(b, t // tt),
        in_specs=[col(0), col(1), col(2), col(3),
                  pl.BlockSpec((1, tt, LANES), lambda i, s: (i, s, gate_blk)),
                  hist(0), hist(1), hist(2), cw(0), cw(1), cw(2), state, row, row, row],
        out_specs=[pl.BlockSpec((1, tt, GDN_VD), lambda i, s: (i, s, 0)), state],
        out_shape=[jax.ShapeDtypeStruct((b, t, GDN_VD), BF16),
                   jax.ShapeDtypeStruct((b, GDN_HEADS, GDN_DK, GDN_DV), F32)],
        scratch_shapes=[pltpu.VMEM((CONV_PAD + tt, GDN_QK), F32)] * 3
        + [pltpu.VMEM((GDN_HEADS, GDN_DK, GDN_DV), F32)],
        compiler_params=_cparams(("arbitrary", "arbitrary")),
        name="gdn_core",
    )(proj, proj, proj, proj, proj, conv_hist, conv_hist, conv_hist, conv_w, conv_w, conv_w,
      s0, alog_row, dtb_row, norm_g.reshape(1, GDN_DV))


def _ssd_kernel(lc, z_ref, xp_ref, bp_ref, cp_ref, dt_ref, hx_ref, hb_ref, hc_ref,
                wx_ref, wb_ref, wc_ref, bx_ref, bb_ref, bc_ref, h0_ref,
                alog_ref, dtb_ref, dskip_ref, ng_ref,
                y_ref, hout_ref,
                xx_s, xb_s, xc_s, h_s):
    t = pl.program_id(2)
    tt = xp_ref.shape[1]
    first = t == 0
    p = SSD_HEADDIM
    gs = SSD_GS
    hrows = SSD_HPG * p
    n_grp = alog_ref.shape[0]

    @pl.when(first)
    def _():
        h_s[...] = h0_ref[0].reshape(n_grp * hrows, SSD_DSTATE)
        for xp_s, hist in ((xx_s, hx_ref), (xb_s, hb_ref), (xc_s, hc_ref)):
            xp_s[0:CONV_H0, :] = jnp.zeros((CONV_H0, xp_s.shape[1]), F32)
            xp_s[CONV_H0:CONV_PAD, :] = hist[0]

    xx_s[CONV_PAD:CONV_PAD + tt, :] = xp_ref[0]
    xb_s[CONV_PAD:CONV_PAD + tt, :] = bp_ref[0]
    xc_s[CONV_PAD:CONV_PAD + tt, :] = cp_ref[0]

    def conv(xp_s, w_ref, b_ref, cols):
        return _silu(_causal_conv(xp_s, w_ref, 0, tt, cols) + b_ref[:, cols])

    shift = lc.bit_length() - 1
    r = lax.broadcasted_iota(I32, (tt, tt), 0)
    c = lax.broadcasted_iota(I32, (tt, tt), 1)
    incl = (lax.shift_right_logical(r, shift) == lax.shift_right_logical(c, shift)) & (r >= c)
    low_half = lax.broadcasted_iota(I32, (tt, LANES), 1) < p

    def per_column(m):
        cols = [jnp.broadcast_to(m[:, e:e + 1], (tt, LANES)) for e in range(SSD_HPG)]
        return jnp.concatenate([jnp.where(low_half, cols[2 * i], cols[2 * i + 1])
                                for i in range(SSD_HPG // 2)], axis=-1)

    def group(k):
        wide = slice(k * gs, (k + 1) * gs)
        lanes = slice(k * LANES, (k + 1) * LANES)
        dt = _softplus(dt_ref[0, :, lanes] + dtb_ref[k])
        gam, gend = _chunk_cumsum(dt * (-jnp.exp(alog_ref[k])), lc)
        yield
        bm = conv(xb_s, wb_ref, bb_ref, lanes)
        gam_t = gam.T
        cm = conv(xc_s, wc_ref, bc_ref, lanes)
        yield
        x = conv(xx_s, wx_ref, bx_ref, wide)
        yield
        eg_x = per_column(jnp.exp(gam))
        xdt = x * per_column(dt)
        xe = xdt * per_column(jnp.exp(gend - gam))
        cb = _mm_nt(cm, bm)
        yield
        parts = []
        for pr in range(SSD_HPG // 2):
            cols = slice(pr * LANES, (pr + 1) * LANES)
            lms = [jnp.exp(jnp.where(incl, gam[:, e:e + 1] - gam_t[e:e + 1, :], -jnp.inf))
                   for e in (2 * pr, 2 * pr + 1)]
            ys = [_mm(cb * lm, xdt[:, cols]) for lm in lms]
            parts.append(jnp.where(low_half, ys[0], ys[1]))
            yield
        y_intra = jnp.concatenate(parts, axis=-1)
        hs = slice(k * hrows, (k + 1) * hrows)
        h = h_s[hs, :]
        inter = []
        for ci in range(tt // lc):
            rs = slice(ci * lc, (ci + 1) * lc)
            inter.append(_mm_nt(cm[rs], h))
            st = _mm_tn(xe[rs], bm[rs])
            cdec = jnp.exp(gend[ci * lc:ci * lc + 1, :])
            h = jnp.concatenate([h[e * p:(e + 1) * p] * cdec[:, e:e + 1] + st[e * p:(e + 1) * p]
                                 for e in range(SSD_HPG)], axis=0)
            yield
        h_s[hs, :] = h
        y = y_intra + jnp.concatenate(inter, axis=0) * eg_x + x * dskip_ref[k]
        y = y * _silu(z_ref[0, :, wide])
        y = y * lax.rsqrt(jnp.mean(y * y, axis=-1, keepdims=True) + EPS) * ng_ref[k]
        y_ref[0, :, wide] = y.astype(y_ref.dtype)

    streams = [group(k) for k in range(n_grp)]
    while streams:
        streams = [s for s in streams if next(s, True) is None]

    xx_s[CONV_H0:CONV_PAD, :] = xx_s[tt + CONV_H0:tt + CONV_PAD, :]
    xb_s[CONV_H0:CONV_PAD, :] = xb_s[tt + CONV_H0:tt + CONV_PAD, :]
    xc_s[CONV_H0:CONV_PAD, :] = xc_s[tt + CONV_H0:tt + CONV_PAD, :]

    @pl.when(t == pl.num_programs(2) - 1)
    def _():
        hout_ref[0] = h_s[...].reshape(n_grp * SSD_HPG, p, SSD_DSTATE)


def _ssd_core(proj, conv_hist, h0, conv_w, conv_b, dt_bias, a_log, d_skip, norm_g):
    b, t, _ = proj.shape
    lc = min(CHUNK, t)
    tt = min(ROW_TILE, t)
    g_n = SSD_GROUPS
    gp = SSD_GROUPS_PER_STEP
    gs = SSD_GS
    wide = gp * gs
    narrow = gp * LANES
    xblk = SSD_INNER // wide
    bblk = 2 * SSD_INNER // narrow
    cblk = bblk + g_n // gp
    dblk = cblk + g_n // gp
    hist_b = SSD_INNER // narrow
    hist_c = hist_b + g_n // gp

    def per_group(v):
        return jnp.pad(v.reshape(g_n, 1, SSD_HPG), ((0, 0), (0, 0), (0, LANES - SSD_HPG)))

    dskip_x = jnp.repeat(d_skip, SSD_HEADDIM).reshape(g_n, 1, gs)
    grow = pl.BlockSpec((gp, 1, LANES), lambda i, g, s: (g, 0, 0))
    cwb = conv_b.reshape(1, SSD_CONV_DIM)
    state = pl.BlockSpec((1, gp * SSD_HPG, SSD_HEADDIM, SSD_DSTATE), lambda i, g, s: (i, g, 0, 0))
    in_specs = [
        pl.BlockSpec((1, tt, wide), lambda i, g, s: (i, s, g)),
        pl.BlockSpec((1, tt, wide), lambda i, g, s: (i, s, xblk + g)),
        pl.BlockSpec((1, tt, narrow), lambda i, g, s: (i, s, bblk + g)),
        pl.BlockSpec((1, tt, narrow), lambda i, g, s: (i, s, cblk + g)),
        pl.BlockSpec((1, tt, narrow), lambda i, g, s: (i, s, dblk + g)),
        pl.BlockSpec((1, CONV_W - 1, wide), lambda i, g, s: (i, 0, g)),
        pl.BlockSpec((1, CONV_W - 1, narrow), lambda i, g, s: (i, 0, hist_b + g)),
        pl.BlockSpec((1, CONV_W - 1, narrow), lambda i, g, s: (i, 0, hist_c + g)),
        pl.BlockSpec((CONV_W, wide), lambda i, g, s: (0, g)),
        pl.BlockSpec((CONV_W, narrow), lambda i, g, s: (0, hist_b + g)),
        pl.BlockSpec((CONV_W, narrow), lambda i, g, s: (0, hist_c + g)),
        pl.BlockSpec((1, wide), lambda i, g, s: (0, g)),
        pl.BlockSpec((1, narrow), lambda i, g, s: (0, hist_b + g)),
        pl.BlockSpec((1, narrow), lambda i, g, s: (0, hist_c + g)),
        state,
        grow, grow,
        pl.BlockSpec((gp, 1, gs), lambda i, g, s: (g, 0, 0)),
        pl.BlockSpec((gp, 1, gs), lambda i, g, s: (g, 0, 0)),
    ]
    return pl.pallas_call(
        functools.partial(_ssd_kernel, lc),
        grid=(b, g_n // gp, t // tt),
        in_specs=in_specs,
        out_specs=[pl.BlockSpec((1, tt, wide), lambda i, g, s: (i, s, g)), state],
        out_shape=[jax.ShapeDtypeStruct((b, t, SSD_INNER), BF16),
                   jax.ShapeDtypeStruct((b, SSD_HEADS, SSD_HEADDIM, SSD_DSTATE), F32)],
        scratch_shapes=[pltpu.VMEM((CONV_PAD + tt, wide), F32),
                        pltpu.VMEM((CONV_PAD + tt, narrow), F32),
                        pltpu.VMEM((CONV_PAD + tt, narrow), F32),
                        pltpu.VMEM((gp * SSD_HPG * SSD_HEADDIM, SSD_DSTATE), F32)],
        compiler_params=_cparams(("arbitrary", "arbitrary", "arbitrary")),
        name="ssd_core",
    )(proj, proj, proj, proj, proj, conv_hist, conv_hist, conv_hist, conv_w, conv_w, conv_w,
      cwb, cwb, cwb, h0, per_group(a_log), per_group(dt_bias), dskip_x,
      norm_g.reshape(g_n, 1, gs))


def _outproj_kernel(o_ref, x_ref, g1_ref, w_ref, n2_ref, sc_ref, sh_ref, wr_ref, br_ref,
                    x1_ref, h2_ref, ri_ref, rw_ref, cnt_ref, cnt_s):
    bb, tt, d = x_ref.shape
    tm = bb * tt
    step = pl.program_id(0) * pl.num_programs(1) + pl.program_id(1)

    @pl.when(step == 0)
    def _():
        cnt_s[...] = jnp.zeros_like(cnt_s)

    out = jnp.dot(o_ref[...].reshape(tm, o_ref.shape[-1]), w_ref[...], preferred_element_type=F32)
    x1 = x_ref[...] + g1_ref[...] * out.reshape(bb, tt, d)
    x1_ref[...] = x1
    y = x1 * lax.rsqrt(jnp.mean(x1 * x1, axis=-1, keepdims=True) + EPS) * n2_ref[...]
    h2 = (y * (1.0 + sc_ref[...]) + sh_ref[...]).reshape(tm, d)
    bits = pltpu.bitcast(h2.astype(BF16).astype(F32), jnp.uint32)
    h2_ref[...] = lax.shift_right_logical(bits[:, :d // 2], jnp.uint32(16)) | (bits[:, d // 2:] & jnp.uint32(0xFFFF0000))

    logits = _mm(h2, wr_ref[...]) + br_ref[...]
    lane = lax.broadcasted_iota(I32, (tm, LANES), 1)
    lane_f = lane.astype(F32)
    neg = -jnp.inf
    gl = jnp.where(lane < MOE_GROUPS, logits, neg)
    ge = jnp.exp(gl - jnp.max(gl, axis=-1, keepdims=True))
    grp_p = ge / jnp.sum(ge, axis=-1, keepdims=True)
    gp = jnp.max(grp_p, axis=-1, keepdims=True)
    gi = jnp.min(jnp.where(grp_p == gp, lane_f, float(LANES)), axis=-1, keepdims=True).astype(I32)
    lo = MOE_GROUPS + gi * MOE_PER_GROUP
    emask = (lane >= lo) & (lane < lo + MOE_PER_GROUP)
    sel = jnp.where(emask, logits, neg)
    se = jnp.exp(sel - jnp.max(sel, axis=-1, keepdims=True))
    p = jnp.where(emask, se / jnp.sum(se, axis=-1, keepdims=True), -1.0)
    v1 = jnp.max(p, axis=-1, keepdims=True)
    i1 = jnp.min(jnp.where(p == v1, lane_f, float(LANES)), axis=-1, keepdims=True).astype(I32)
    p2 = jnp.where(lane == i1, -1.0, p)
    v2 = jnp.max(p2, axis=-1, keepdims=True)
    i2 = jnp.min(jnp.where(p2 == v2, lane_f, float(LANES)), axis=-1, keepdims=True).astype(I32)
    den = v1 + v2
    w1 = gp * v1 / den
    w2 = gp * v2 / den
    e1 = i1 - MOE_GROUPS
    e2 = i2 - MOE_GROUPS

    hit1 = lane == e1
    hit2 = lane == e2
    onehot = jnp.where(hit1 | hit2, 1.0, 0.0)
    r = lax.broadcasted_iota(I32, (tm, tm), 0)
    c = lax.broadcasted_iota(I32, (tm, tm), 1)
    before = _mm(jnp.where(r > c, 1.0, 0.0), onehot) + cnt_s[...]
    r1 = jnp.sum(jnp.where(hit1, before, 0.0), axis=-1, keepdims=True).astype(I32)
    r2 = jnp.sum(jnp.where(hit2, before, 0.0), axis=-1, keepdims=True).astype(I32)
    cnt_s[...] = cnt_s[...] + jnp.sum(onehot, axis=0, keepdims=True)
    cnt_ref[...] = cnt_s[...].astype(I32)
    ri_ref[...] = jnp.where(lane == 0, e1, jnp.where(lane == 1, e2, jnp.where(lane == 2, r1, jnp.where(lane == 3, r2, 0))))
    rw_ref[...] = jnp.where(lane == 0, w1, jnp.where(lane == 1, w2, 0.0))


def _out_proj_route(o, x, mod, w_bf16, norm2_g, w_router, b_router):
    b, t, d = x.shape
    kdim = o.shape[-1]
    bb, tt = _tiles(b, t)
    tm = bb * tt
    nt = t // tt
    tok = lambda i, j: (i * nt + j, 0)
    modspec = lambda idx: pl.BlockSpec((bb, 1, d), lambda i, j: (i, 0, idx))
    return pl.pallas_call(
        _outproj_kernel,
        grid=(b // bb, nt),
        in_specs=[pl.BlockSpec((bb, tt, kdim), lambda i, j: (i, j, 0)),
                  pl.BlockSpec((bb, tt, d), lambda i, j: (i, j, 0)),
                  modspec(2),
                  pl.BlockSpec((kdim, d), lambda i, j: (0, 0)),
                  pl.BlockSpec((1, d), lambda i, j: (0, 0)),
                  modspec(4), modspec(3),
                  pl.BlockSpec((d, LANES), lambda i, j: (0, 0)),
                  pl.BlockSpec((1, LANES), lambda i, j: (0, 0))],
        out_specs=[pl.BlockSpec((bb, tt, d), lambda i, j: (i, j, 0)),
                   pl.BlockSpec((tm, d // 2), tok),
                   pl.BlockSpec((tm, LANES), tok),
                   pl.BlockSpec((tm, LANES), tok),
                   pl.BlockSpec((1, LANES), lambda i, j: (0, 0))],
        out_shape=[jax.ShapeDtypeStruct((b, t, d), F32),
                   jax.ShapeDtypeStruct((b * t, d // 2), jnp.uint32),
                   jax.ShapeDtypeStruct((b * t, LANES), I32),
                   jax.ShapeDtypeStruct((b * t, LANES), F32),
                   jax.ShapeDtypeStruct((1, LANES), I32)],
        scratch_shapes=[pltpu.VMEM((1, LANES), F32)],
        compiler_params=_cparams(("arbitrary", "arbitrary")),
        name="out_proj_route",
    )(o, x, mod, w_bf16, norm2_g.reshape(1, d), mod, mod, w_router, b_router)


def _dispatch_kernel(dest_ref, h2_ref, xb_in_ref, xb_ref, sem):
    del xb_in_ref
    tm = h2_ref.shape[0]

    def copy(r, k):
        return pltpu.make_async_copy(h2_ref.at[pl.ds(r, 1), :],
                                     xb_ref.at[pl.ds(dest_ref[0, 0, MOE_TOPK * r + k], 1), :], sem)

    for r in range(tm):
        for k in range(MOE_TOPK):
            copy(r, k).start()
    for r in range(tm):
        for k in range(MOE_TOPK):
            copy(r, k).wait()


def _dispatch(h2, dest_tiles, cap):
    n, d = h2.shape
    nt, _, per = dest_tiles.shape
    tm = per // MOE_TOPK
    return pl.pallas_call(
        _dispatch_kernel,
        grid=(nt,),
        in_specs=[pl.BlockSpec((1, 1, per), lambda i: (i, 0, 0), memory_space=pltpu.SMEM),
                  pl.BlockSpec((tm, d), lambda i: (i, 0)),
                  pl.BlockSpec(memory_space=pl.ANY)],
        out_specs=pl.BlockSpec(memory_space=pl.ANY),
        out_shape=jax.ShapeDtypeStruct((cap, d), h2.dtype),
        scratch_shapes=[pltpu.SemaphoreType.DMA],
        input_output_aliases={2: 0},
        compiler_params=_cparams(("arbitrary",)),
        name="moe_dispatch",
    )(dest_tiles, h2, jnp.zeros((cap, d), h2.dtype))


def _expert_kernel(be_ref, nu_ref, x_ref, wg_ref, wu_ref, wd_ref, y_ref):
    used = pl.program_id(0) < nu_ref[0]

    @pl.when(used)
    def _():
        words = x_ref[...]
        x = jnp.concatenate([pltpu.bitcast(lax.shift_left(words, jnp.uint32(16)), F32),
                             pltpu.bitcast(words & jnp.uint32(0xFFFF0000), F32)], axis=-1).astype(BF16)
        hid = _silu(_mm(x, wg_ref[0, 0])) * _mm(x, wu_ref[0, 0])
        y_ref[...] = _mm(hid, wd_ref[0, 0])

    @pl.when(jnp.logical_not(used))
    def _():
        y_ref[...] = jnp.zeros_like(y_ref)


def _experts(xb, rows, block_e, n_used, layer, w_gate, w_up, w_down):
    cap, half = xb.shape
    d = 2 * half
    nb = cap // rows
    last = lambda i, nu: jnp.maximum(jnp.minimum(i, nu[0] - 1), 0)
    blk = lambda i, be, nu: (last(i, nu), 0)
    wsel = lambda i, be, nu: (layer, be[last(i, nu)], 0, 0)
    return pl.pallas_call(
        _expert_kernel,
        grid_spec=pltpu.PrefetchScalarGridSpec(
            num_scalar_prefetch=2,
            grid=(nb,),
            in_specs=[pl.BlockSpec((rows, half), blk),
                      pl.BlockSpec((1, 1, d, D_EXPERT), wsel),
                      pl.BlockSpec((1, 1, d, D_EXPERT), wsel),
                      pl.BlockSpec((1, 1, D_EXPERT, d), wsel)],
            out_specs=pl.BlockSpec((rows, d), lambda i, be, nu: (i, 0))),
        out_shape=jax.ShapeDtypeStruct((cap, d), F32),
        compiler_params=_cparams(("arbitrary",)),
        name="moe_experts",
    )(block_e, n_used, xb, w_gate, w_up, w_down)


def _combine_kernel(final, dest_ref, x1_ref, rw_ref, g2_ref, fg_ref, yb_ref, o_ref, buf_s, sem):
    bb, tt, d = x1_ref.shape
    tm = bb * tt

    def copy(r, k):
        return pltpu.make_async_copy(yb_ref.at[pl.ds(dest_ref[0, 0, MOE_TOPK * r + k], 1), :],
                                     buf_s.at[k, pl.ds(r, 1), :], sem)

    for r in range(tm):
        for k in range(MOE_TOPK):
            copy(r, k).start()
    for r in range(tm):
        for k in range(MOE_TOPK):
            copy(r, k).wait()
    rw = rw_ref[...]
    moe = buf_s[0] * rw[:, 0:1] + buf_s[1] * rw[:, 1:2]
    x2 = x1_ref[...] + g2_ref[...] * moe.reshape(bb, tt, d)
    if final:
        x2 = x2 * lax.rsqrt(jnp.mean(x2 * x2, axis=-1, keepdims=True) + EPS) * fg_ref[...]
    o_ref[...] = x2


def _combine(x1, route_w, mod, yb, dest_tiles, final_g, final):
    b, t, d = x1.shape
    bb, tt = _tiles(b, t, MOE_MOVE_ROWS)
    tm = bb * tt
    nt = t // tt
    per = dest_tiles.shape[-1]
    return pl.pallas_call(
        functools.partial(_combine_kernel, final),
        grid=(b // bb, nt),
        in_specs=[pl.BlockSpec((1, 1, per), lambda i, j: (i * nt + j, 0, 0), memory_space=pltpu.SMEM),
                  pl.BlockSpec((bb, tt, d), lambda i, j: (i, j, 0)),
                  pl.BlockSpec((tm, LANES), lambda i, j: (i * nt + j, 0)),
                  pl.BlockSpec((bb, 1, d), lambda i, j: (i, 0, 5)),
                  pl.BlockSpec((1, d), lambda i, j: (0, 0)),
                  pl.BlockSpec(memory_space=pl.ANY)],
        out_specs=pl.BlockSpec((bb, tt, d), lambda i, j: (i, j, 0)),
        out_shape=jax.ShapeDtypeStruct((b, t, d), F32),
        scratch_shapes=[pltpu.VMEM((MOE_TOPK, tm, d), F32), pltpu.SemaphoreType.DMA],
        compiler_params=_cparams(("arbitrary", "arbitrary")),
        name="moe_combine",
    )(dest_tiles, x1, route_w, mod, final_g.reshape(1, d), yb)


def _moe(layer, x1, h2, route_i, route_w, counts, mod, w_gate, w_up, w_down, final_g, final):
    b, t, d = x1.shape
    n = b * t
    bb, tt = _tiles(b, t, MOE_MOVE_ROWS)
    tm = bb * tt
    n_asg = n * MOE_TOPK
    rows = min(EXPERT_ROWS_MAX, max(EXPERT_ROWS_MIN, pl.next_power_of_2(n_asg // (2 * N_EXPERTS))))
    nb = (n_asg + N_EXPERTS * (rows - 1) + rows - 1) // rows
    cap = nb * rows
    cnt = counts[0, :N_EXPERTS]
    padded = (cnt + rows - 1) // rows * rows
    ends = jnp.cumsum(padded)
    pstart = ends - padded
    eid = route_i[:, :MOE_TOPK]
    rank = route_i[:, MOE_TOPK:2 * MOE_TOPK]
    first_slot = jnp.sum(jnp.where(eid[..., None] == jnp.arange(N_EXPERTS, dtype=I32), pstart, 0), axis=-1)
    dest_tiles = (first_slot + rank).astype(I32).reshape(n // tm, 1, tm * MOE_TOPK)
    blk_start = jnp.arange(nb, dtype=I32) * rows
    block_e = jnp.minimum(jnp.sum(blk_start[:, None] >= ends[None, :], axis=-1), N_EXPERTS - 1).astype(I32)
    n_used = (ends[-1:] // rows).astype(I32)
    xb = _dispatch(h2, dest_tiles, cap)
    yb = _experts(xb, rows, block_e, n_used, layer, w_gate, w_up, w_down)
    return _combine(x1, route_w, mod, yb, dest_tiles, final_g, final)


def _pad_cols(w, n):
    return jnp.pad(w, ((0, 0), (0, n - w.shape[1])))


def _trunk(x, mods, gdn_conv, gdn_ssm, ssd_conv, ssd_ssm, w):
    b, t, d = x.shape
    gdn_main = GDN_CONV_DIM + GDN_VD
    w_in = jnp.concatenate([w['gdn_w_in'][0][:, :gdn_main], _pad_cols(w['gdn_w_in'][0][:, gdn_main:], LANES)],
                           axis=1).astype(BF16)
    proj = _in_proj(x, w['norm1_g'][0], mods[0], 1, 0, w_in)
    o, gdn_state = _gdn_core(proj, gdn_conv[0], gdn_ssm[0], w['gdn_conv_w'][0], w['gdn_A_log'][0],
                             w['gdn_dt_bias'][0], w['gdn_norm_g'][0])
    gdn_hist = proj[:, t - (CONV_W - 1):, :GDN_CONV_DIM]
    x = _layer_tail(0, o, x, mods[0], w['gdn_w_out'][0], w, False)
    ssd_main = SSD_INNER + SSD_CONV_DIM
    dt_cols = [_pad_cols(w['ssd_w_in'][0][:, ssd_main + g * SSD_HPG: ssd_main + (g + 1) * SSD_HPG], LANES)
               for g in range(SSD_GROUPS)]
    w_in = jnp.concatenate([w['ssd_w_in'][0][:, :ssd_main]] + dt_cols, axis=1).astype(BF16)
    proj = _in_proj(x, w['norm1_g'][1], mods[1], 1, 0, w_in)
    y, ssd_state = _ssd_core(proj, ssd_conv[0], ssd_ssm[0], w['ssd_conv_w'][0], w['ssd_conv_b'][0],
                             w['ssd_dt_bias'][0], w['ssd_A_log'][0], w['ssd_D'][0], w['ssd_norm_g'][0])
    ssd_hist = proj[:, t - (CONV_W - 1):, SSD_INNER:ssd_main]
    y_out = _layer_tail(1, y, x, mods[1], w['ssd_w_out'][0], w, True)
    return y_out, gdn_hist[None], gdn_state[None], ssd_hist[None], ssd_state[None]


def _layer_tail(i, mixed, x, mod, w_out, w, final):
    d = x.shape[-1]
    w_router = _pad_cols(jnp.concatenate([w['moe_w_group'][i], w['moe_w_expert'][i]], axis=1), LANES)
    b_router = _pad_cols(jnp.concatenate([w['moe_b_group'][i], w['moe_b_expert'][i]]).reshape(1, -1), LANES)
    x1, h2, route_i, route_w, counts = _out_proj_route(mixed, x, mod, w_out.astype(BF16), w['norm2_g'][i],
                                                       w_router, b_router)
    del d
    return _moe(i, x1, h2, route_i, route_w, counts, mod, w['moe_w_gate'], w['moe_w_up'], w['moe_w_down'],
                w['final_norm_g'], final)


def kernel(x_prompt, x_sample, state_gdn_conv, state_gdn_ssm, state_ssd_conv, state_ssd_ssm, c_prompt, c_sample,
           norm1_g, norm2_g, ada_w, ada_b, gdn_w_in, gdn_conv_w, gdn_A_log, gdn_dt_bias, gdn_norm_g, gdn_w_out,
           ssd_w_in, ssd_conv_w, ssd_conv_b, ssd_dt_bias, ssd_A_log, ssd_D, ssd_norm_g, ssd_w_out,
           moe_w_group, moe_b_group, moe_w_expert, moe_b_expert, moe_w_gate, moe_w_up, moe_w_down, final_norm_g):
    w = {'norm1_g': norm1_g, 'norm2_g': norm2_g, 'gdn_w_in': gdn_w_in, 'gdn_conv_w': gdn_conv_w,
         'gdn_A_log': gdn_A_log, 'gdn_dt_bias': gdn_dt_bias, 'gdn_norm_g': gdn_norm_g, 'gdn_w_out': gdn_w_out,
         'ssd_w_in': ssd_w_in, 'ssd_conv_w': ssd_conv_w, 'ssd_conv_b': ssd_conv_b, 'ssd_dt_bias': ssd_dt_bias,
         'ssd_A_log': ssd_A_log, 'ssd_D': ssd_D, 'ssd_norm_g': ssd_norm_g, 'ssd_w_out': ssd_w_out,
         'moe_w_group': moe_w_group, 'moe_b_group': moe_b_group, 'moe_w_expert': moe_w_expert,
         'moe_b_expert': moe_b_expert, 'moe_w_gate': moe_w_gate, 'moe_w_up': moe_w_up, 'moe_w_down': moe_w_down,
         'final_norm_g': final_norm_g}
    bp = x_prompt.shape[0]
    bs = x_sample.shape[0]
    dt_ = x_prompt.dtype
    n_seq = bp + bs
    n_pad = -n_seq % SUBLANES
    c_all = jnp.concatenate([c_prompt, c_sample, jnp.zeros((n_pad, c_prompt.shape[1]), dt_)], axis=0)
    mod_all = _ada_mod(c_all, ada_w, ada_b)
    mods_p = [mod_all[l, :bp][:, None, :] for l in range(DEPTH)]
    mods_s = [mod_all[l, bp:n_seq][:, None, :] for l in range(DEPTH)]
    n_gdn = state_gdn_conv.shape[0]
    n_ssd = state_ssd_conv.shape[0]
    z_gc = jnp.zeros((n_gdn, bp) + state_gdn_conv.shape[2:], dt_)
    z_gs = jnp.zeros((n_gdn, bp) + state_gdn_ssm.shape[2:], dt_)
    z_sc = jnp.zeros((n_ssd, bp) + state_ssd_conv.shape[2:], dt_)
    z_ss = jnp.zeros((n_ssd, bp) + state_ssd_ssm.shape[2:], dt_)
    y_p, p_gc, p_gs, p_sc, p_ss = _trunk(x_prompt, mods_p, z_gc, z_gs, z_sc, z_ss, w)
    y_s, s_gc, s_gs, s_sc, s_ss = _trunk(x_sample, mods_s, state_gdn_conv, state_gdn_ssm,
                                         state_ssd_conv, state_ssd_ssm, w)
    return (y_p, y_s, p_gc, p_gs, p_sc, p_ss, s_gc, s_gs, s_sc, s_ss)
```

```python
import functools

import jax
import jax.numpy as jnp
from jax import lax
from jax.experimental import pallas as pl
from jax.experimental.pallas import tpu as pltpu

F32 = jnp.float32
BF16 = jnp.bfloat16
I32 = jnp.int32

D_MODEL = 1024
DEPTH = 2
CHUNK = 64
CONV_W = 4
EPS = 1e-6
GDN_HEADS = 8
GDN_DK = 128
GDN_DV = 128
GDN_QK = GDN_HEADS * GDN_DK
GDN_VD = GDN_HEADS * GDN_DV
GDN_CONV_DIM = 2 * GDN_QK + GDN_VD
SSD_INNER = 2 * D_MODEL
SSD_HEADDIM = 64
SSD_HEADS = SSD_INNER // SSD_HEADDIM
SSD_GROUPS = 4
SSD_HPG = SSD_HEADS // SSD_GROUPS
SSD_DSTATE = 128
SSD_GS = SSD_INNER // SSD_GROUPS
SSD_CONV_DIM = SSD_INNER + 2 * SSD_GROUPS * SSD_DSTATE
MOE_GROUPS = 4
MOE_PER_GROUP = 8
N_EXPERTS = MOE_GROUPS * MOE_PER_GROUP
MOE_TOPK = 2
D_EXPERT = 512

LANES = 128
SUBLANES = 8
VMEM_LIMIT = 48 * 1024 * 1024

ROW_TILE = 256
ROUTE_ROWS = 512
MOE_MOVE_ROWS = 1024
EXPERT_ROWS_MIN = 128
EXPERT_ROWS_MAX = 512
CONV_PAD = SUBLANES
SSD_GROUPS_PER_STEP = 4


def _cparams(sem):
    return pltpu.CompilerParams(dimension_semantics=sem, vmem_limit_bytes=VMEM_LIMIT)


def _mm(a, b):
    return jnp.dot(a.astype(BF16), b.astype(BF16), preferred_element_type=F32)


def _mm_nt(a, b):
    return lax.dot_general(a.astype(BF16), b.astype(BF16), (((1,), (1,)), ((), ())),
                           preferred_element_type=F32)


def _mm_tn(a, b):
    return lax.dot_general(a.astype(BF16), b.astype(BF16), (((0,), (0,)), ((), ())),
                           preferred_element_type=F32)


def _mm_f32(a, b):
    return jnp.dot(a, b, preferred_element_type=F32, precision=lax.Precision.HIGHEST)


def _silu(x):
    h = 0.5 * x
    return h + h * jnp.tanh(h)


def _softplus(x):
    return jnp.maximum(x, 0.0) + jnp.log1p(jnp.exp(-jnp.abs(x)))


def _tiles(b, t, rows=ROW_TILE):
    rows = min(rows, b * t)
    if t >= rows:
        assert t % rows == 0
        return 1, rows
    assert rows % t == 0 and b % (rows // t) == 0
    return rows // t, t


def _ada_kernel(c_ref, w_ref, b_ref, o_ref):
    o_ref[0] = _mm(_silu(c_ref[...]), w_ref[0]) + b_ref[0]


def _ada_mod(c_all, ada_w, ada_b):
    bp, d = c_all.shape
    n = ada_w.shape[-1]
    tn = 1024
    return pl.pallas_call(
        _ada_kernel,
        grid=(DEPTH, n // tn),
        in_specs=[pl.BlockSpec((bp, d), lambda l, j: (0, 0)),
                  pl.BlockSpec((1, d, tn), lambda l, j: (l, 0, j)),
                  pl.BlockSpec((1, 1, tn), lambda l, j: (l, 0, j))],
        out_specs=pl.BlockSpec((1, bp, tn), lambda l, j: (l, 0, j)),
        out_shape=jax.ShapeDtypeStruct((DEPTH, bp, n), F32),
        compiler_params=_cparams(("arbitrary", "arbitrary")),
        name="ada_mod",
    )(c_all, ada_w, ada_b.reshape(DEPTH, 1, n))


def _inproj_kernel(x_ref, g_ref, sc_ref, sh_ref, w_ref, o_ref):
    bb, tt, d = x_ref.shape
    x = x_ref[...]
    y = x * lax.rsqrt(jnp.mean(x * x, axis=-1, keepdims=True) + EPS) * g_ref[...]
    h = y * (1.0 + sc_ref[...]) + sh_ref[...]
    o = _mm(h.reshape(bb * tt, d), w_ref[...])
    o_ref[...] = o.reshape(bb, tt, o.shape[-1])


def _in_proj(x, norm_g, mod, sc_idx, sh_idx, w_bf16):
    b, t, d = x.shape
    n = w_bf16.shape[1]
    bb, tt = _tiles(b, t)
    return pl.pallas_call(
        _inproj_kernel,
        grid=(b // bb, t // tt),
        in_specs=[pl.BlockSpec((bb, tt, d), lambda i, j: (i, j, 0)),
                  pl.BlockSpec((1, d), lambda i, j: (0, 0)),
                  pl.BlockSpec((bb, 1, d), lambda i, j: (i, 0, sc_idx)),
                  pl.BlockSpec((bb, 1, d), lambda i, j: (i, 0, sh_idx)),
                  pl.BlockSpec((d, n), lambda i, j: (0, 0))],
        out_specs=pl.BlockSpec((bb, tt, n), lambda i, j: (i, j, 0)),
        out_shape=jax.ShapeDtypeStruct((b, t, n), F32),
        compiler_params=_cparams(("arbitrary", "arbitrary")),
        name="in_proj",
    )(x, norm_g.reshape(1, d), mod, mod, w_bf16)


CONV_H0 = CONV_PAD - (CONV_W - 1)


def _causal_conv(xp_ref, w_ref, row0, n_rows, cols):
    e = xp_ref[row0:row0 + CONV_PAD + n_rows, cols]
    acc = e * w_ref[0:1, cols]
    for i in range(1, CONV_W):
        acc = pltpu.roll(acc, 1, 0) + e * w_ref[i:i + 1, cols]
    return acc[CONV_PAD:]


def _chunk_cumsum(v, lc):
    tt = v.shape[0]
    shift = lc.bit_length() - 1
    r = lax.broadcasted_iota(I32, (tt, tt), 0)
    c = lax.broadcasted_iota(I32, (tt, tt), 1)
    same = lax.shift_right_logical(r, shift) == lax.shift_right_logical(c, shift)
    tri = jnp.where(same & (r >= c), 1.0, 0.0).astype(BF16)
    hi = v.astype(BF16)
    r1 = v - hi.astype(F32)
    mid = r1.astype(BF16)
    lo = (r1 - mid.astype(F32)).astype(BF16)
    dot = functools.partial(jnp.dot, preferred_element_type=F32)
    cum = dot(tri, hi) + dot(tri, mid) + dot(tri, lo)
    total = jnp.concatenate([jnp.broadcast_to(cum[e - 1:e, :], (lc, v.shape[1])) for e in range(lc, tt + 1, lc)],
                            axis=0)
    return cum, total


def _split_bf16(a):
    hi = a.astype(BF16)
    return hi, (a - hi.astype(F32)).astype(BF16)


def _mm_split(a, b):
    ah, al = _split_bf16(a)
    bh, bl = _split_bf16(b)
    dot = functools.partial(jnp.dot, preferred_element_type=F32)
    return dot(ah, bh) + dot(ah, bl) + dot(al, bh)


def _gdn_kernel(lc, qp_ref, kp_ref, vp_ref, z_ref, gt_ref, hq_ref, hk_ref, hv_ref,
                wq_ref, wk_ref, wv_ref, s0_ref, alog_ref, dtb_ref, ng_ref,
                o_ref, sout_ref,
                xq_s, xk_s, xv_s, st_s):
    t = pl.program_id(1)
    tt = qp_ref.shape[1]
    first = t == 0
    n_chunks = tt // lc
    heads = range(GDN_HEADS)
    sls = [slice(h * GDN_DK, (h + 1) * GDN_DK) for h in heads]

    @pl.when(first)
    def _():
        st_s[...] = s0_ref[0]
        for xp_s, hist in ((xq_s, hq_ref), (xk_s, hk_ref), (xv_s, hv_ref)):
            xp_s[0:CONV_H0, :] = jnp.zeros((CONV_H0, xp_s.shape[1]), F32)
            xp_s[CONV_H0:CONV_PAD, :] = hist[0]

    gates = gt_ref[0]
    beta = jax.nn.sigmoid(gates)
    g = -jnp.exp(alog_ref[...]) * _softplus(gates + dtb_ref[...])
    gam, gend = _chunk_cumsum(g, lc)
    egam = jnp.exp(gam)
    eend = jnp.exp(gend - gam)
    rw = lax.broadcasted_iota(I32, (lc, LANES), 0)
    cw = lax.broadcasted_iota(I32, (lc, LANES), 1)
    incl = (rw >= cw) & (cw < lc)
    strict = (rw > cw) & (cw < lc)
    right = (cw >= lc) & (cw < 2 * lc)
    eye_right = jnp.where(cw == rw + lc, 1.0, 0.0)

    xq_s[CONV_PAD:CONV_PAD + tt, :] = qp_ref[0]
    xk_s[CONV_PAD:CONV_PAD + tt, :] = kp_ref[0]
    xv_s[CONV_PAD:CONV_PAD + tt, :] = vp_ref[0]

    def conv_head(xp_ref, w_ref, ci, sl):
        return _silu(_causal_conv(xp_ref, w_ref, ci * lc, lc, sl))

    def prepare(ci, h):
        qh = conv_head(xq_s, wq_ref, ci, sls[h])
        kh = conv_head(xk_s, wk_ref, ci, sls[h])
        vh = conv_head(xv_s, wv_ref, ci, sls[h])
        qh = qh * (lax.rsqrt(jnp.sum(qh * qh, axis=-1, keepdims=True) + EPS) * (GDN_DK ** -0.5))
        kh = kh * lax.rsqrt(jnp.sum(kh * kh, axis=-1, keepdims=True) + EPS)
        return qh, kh, vh

    def recur(ci, qkv):
        rows = slice(ci * lc, (ci + 1) * lc)
        qs = [qkv[h][0] for h in heads]
        ks = [qkv[h][1] for h in heads]
        vs = [qkv[h][2] for h in heads]
        gam_c = gam[rows, :]
        gam_t = jnp.concatenate([gam_c, jnp.zeros((LANES - lc, LANES), F32)], axis=0).T
        b_cols = [beta[rows, h:h + 1] for h in heads]
        egs = [egam[rows, GDN_HEADS + h:GDN_HEADS + h + 1] for h in heads]
        dec_incl = [jnp.exp(jnp.where(incl, gam_c[:, GDN_HEADS + h:GDN_HEADS + h + 1]
                                      - gam_t[GDN_HEADS + h:GDN_HEADS + h + 1, :], -jnp.inf)) for h in heads]
        zrow = jnp.zeros((LANES - lc, GDN_DK), F32)
        qkk = [_mm_nt(jnp.concatenate([qs[h], ks[h]], axis=0), jnp.concatenate([ks[h], zrow], axis=0))
               for h in heads]
        qk = [qkk[h][:lc] * dec_incl[h] for h in heads]
        a = [b_cols[h] * jnp.where(strict, dec_incl[h], 0.0) * qkk[h][lc:] for h in heads]
        yield
        cs = [eye_right - m for m in a]
        n = 1
        while n < lc:
            nxt_cs = []
            for cm in cs:
                ch, cl = _split_bf16(cm)
                ph, pl_ = ch[:, :lc], cl[:, :lc]
                dot = functools.partial(jnp.dot, preferred_element_type=F32)
                nxt_cs.append(dot(ph, ch) + dot(ph, cl) + dot(pl_, ch) + jnp.where(right, cm, 0.0))
            cs = nxt_cs
            n *= 2
            yield
        pad = [jnp.zeros((n_rows, GDN_DV + GDN_DK), F32) for n_rows in (lc, LANES - 2 * lc) if n_rows]
        rhs = [jnp.concatenate([pad[0], jnp.concatenate([b_cols[h] * vs[h], (b_cols[h] * egs[h]) * ks[h]], axis=-1)]
                               + pad[1:], axis=0) for h in heads]
        x = [_mm(cs[h], rhs[h]) for h in heads]
        yield
        s = [st_s[h] for h in heads]
        both = [_mm(jnp.concatenate([x[h][:, GDN_DV:], qs[h] * egs[h]], axis=0), s[h]) for h in heads]
        w = [x[h][:, :GDN_DV] - both[h][:lc] for h in heads]
        o = [both[h][lc:] + _mm(qk[h][:, :lc], w[h]) for h in heads]
        yield
        cdec = jnp.exp(gend[ci * lc:ci * lc + 1, :])
        for h in heads:
            gl = GDN_HEADS + h
            k_end = ks[h] * eend[rows, gl:gl + 1]
            st_s[h] = cdec[:, gl:gl + 1] * s[h] + _mm_tn(k_end, w[h])
        yield
        for h in heads:
            on = o[h] * lax.rsqrt(jnp.mean(o[h] * o[h], axis=-1, keepdims=True) + EPS) * ng_ref[...]
            o_ref[0, rows, sls[h]] = (on * _silu(z_ref[0, rows, sls[h]])).astype(o_ref.dtype)

    ready = [prepare(0, h) for h in heads]
    for ci in range(n_chunks):
        todo = iter(heads if ci + 1 < n_chunks else ())
        nxt = []
        for _ in recur(ci, ready):
            h = next(todo, None)
            if h is not None:
                nxt.append(prepare(ci + 1, h))
        nxt.extend(prepare(ci + 1, h) for h in todo)
        ready = nxt

    xq_s[CONV_H0:CONV_PAD, :] = xq_s[tt + CONV_H0:tt + CONV_PAD, :]
    xk_s[CONV_H0:CONV_PAD, :] = xk_s[tt + CONV_H0:tt + CONV_PAD, :]
    xv_s[CONV_H0:CONV_PAD, :] = xv_s[tt + CONV_H0:tt + CONV_PAD, :]

    @pl.when(t == pl.num_programs(1) - 1)
    def _():
        sout_ref[0] = st_s[...]


def _gdn_core(proj, conv_hist, s0, conv_w, a_log, dt_bias, norm_g):
    b, t, _ = proj.shape
    lc = min(CHUNK, t)
    tt = min(ROW_TILE, t)
    nq = GDN_QK // GDN_QK
    lane_pad = jnp.zeros((LANES - 2 * GDN_HEADS,), F32)
    alog_row = jnp.concatenate([jnp.zeros((GDN_HEADS,), F32), a_log, lane_pad]).reshape(1, LANES)
    dtb_row = jnp.concatenate([jnp.zeros((GDN_HEADS,), F32), dt_bias, lane_pad]).reshape(1, LANES)
    gate_blk = (GDN_CONV_DIM + GDN_VD) // LANES
    col = lambda j: pl.BlockSpec((1, tt, GDN_QK), lambda i, s: (i, s, j))
    hist = lambda j: pl.BlockSpec((1, CONV_W - 1, GDN_QK), lambda i, s: (i, 0, j))
    cw = lambda j: pl.BlockSpec((CONV_W, GDN_QK), lambda i, s: (0, j))
    row = pl.BlockSpec((1, LANES), lambda i, s: (0, 0))
    state = pl.BlockSpec((1, GDN_HEADS, GDN_DK, GDN_DV), lambda i, s: (i, 0, 0, 0))
    del nq
    return pl.pallas_call(
        functools.partial(_gdn_kernel, lc),
        grid=(b, t // tt),
        in_specs=[col(0), col(1), col(2), col(3),
                  pl.BlockSpec((1, tt, LANES), lambda i, s: (i, s, gate_blk)),
                  hist(0), hist(1), hist(2), cw(0), cw(1), cw(2), state, row, row, row],
        out_specs=[pl.BlockSpec((1, tt, GDN_VD), lambda i, s: (i, s, 0)), state],
        out_shape=[jax.ShapeDtypeStruct((b, t, GDN_VD), BF16),
                   jax.ShapeDtypeStruct((b, GDN_HEADS, GDN_DK, GDN_DV), F32)],
        scratch_shapes=[pltpu.VMEM((CONV_PAD + tt, GDN_QK), F32)] * 3
        + [pltpu.VMEM((GDN_HEADS, GDN_DK, GDN_DV), F32)],
        compiler_params=_cparams(("arbitrary", "arbitrary")),
        name="gdn_core",
    )(proj, proj, proj, proj, proj, conv_hist, conv_hist, conv_hist, conv_w, conv_w, conv_w,
      s0, alog_row, dtb_row, norm_g.reshape(1, GDN_DV))


def _ssd_kernel(lc, z_ref, xp_ref, bp_ref, cp_ref, dt_ref, hx_ref, hb_ref, hc_ref,
                wx_ref, wb_ref, wc_ref, bx_ref, bb_ref, bc_ref, h0_ref,
                alog_ref, dtb_ref, dskip_ref, ng_ref,
                y_ref, hout_ref,
                xx_s, xb_s, xc_s, h_s):
    t = pl.program_id(2)
    tt = xp_ref.shape[1]
    first = t == 0
    p = SSD_HEADDIM
    gs = SSD_GS
    hrows = SSD_HPG * p
    n_grp = alog_ref.shape[0]

    @pl.when(first)
    def _():
        h_s[...] = h0_ref[0].reshape(n_grp * hrows, SSD_DSTATE)
        for xp_s, hist in ((xx_s, hx_ref), (xb_s, hb_ref), (xc_s, hc_ref)):
            xp_s[0:CONV_H0, :] = jnp.zeros((CONV_H0, xp_s.shape[1]), F32)
            xp_s[CONV_H0:CONV_PAD, :] = hist[0]

    xx_s[CONV_PAD:CONV_PAD + tt, :] = xp_ref[0]
    xb_s[CONV_PAD:CONV_PAD + tt, :] = bp_ref[0]
    xc_s[CONV_PAD:CONV_PAD + tt, :] = cp_ref[0]

    def conv(xp_s, w_ref, b_ref, cols):
        return _silu(_causal_conv(xp_s, w_ref, 0, tt, cols) + b_ref[:, cols])

    shift = lc.bit_length() - 1
    r = lax.broadcasted_iota(I32, (tt, tt), 0)
    c = lax.broadcasted_iota(I32, (tt, tt), 1)
    incl = (lax.shift_right_logical(r, shift) == lax.shift_right_logical(c, shift)) & (r >= c)
    low_half = lax.broadcasted_iota(I32, (tt, LANES), 1) < p

    def per_column(m):
        cols = [jnp.broadcast_to(m[:, e:e + 1], (tt, LANES)) for e in range(SSD_HPG)]
        return jnp.concatenate([jnp.where(low_half, cols[2 * i], cols[2 * i + 1])
                                for i in range(SSD_HPG // 2)], axis=-1)

    def group(k):
        wide = slice(k * gs, (k + 1) * gs)
        lanes = slice(k * LANES, (k + 1) * LANES)
        dt = _softplus(dt_ref[0, :, lanes] + dtb_ref[k])
        gam, gend = _chunk_cumsum(dt * (-jnp.exp(alog_ref[k])), lc)
        yield
        bm = conv(xb_s, wb_ref, bb_ref, lanes)
        gam_t = gam.T
        cm = conv(xc_s, wc_ref, bc_ref, lanes)
        yield
        x = conv(xx_s, wx_ref, bx_ref, wide)
        yield
        eg_x = per_column(jnp.exp(gam))
        xdt = x * per_column(dt)
        xe = xdt * per_column(jnp.exp(gend - gam))
        cb = _mm_nt(cm, bm)
        yield
        parts = []
        for pr in range(SSD_HPG // 2):
            cols = slice(pr * LANES, (pr + 1) * LANES)
            lms = [jnp.exp(jnp.where(incl, gam[:, e:e + 1] - gam_t[e:e + 1, :], -jnp.inf))
                   for e in (2 * pr, 2 * pr + 1)]
            ys = [_mm(cb * lm, xdt[:, cols]) for lm in lms]
            parts.append(jnp.where(low_half, ys[0], ys[1]))
            yield
        y_intra = jnp.concatenate(parts, axis=-1)
        hs = slice(k * hrows, (k + 1) * hrows)
        h = h_s[hs, :]
        inter = []
        for ci in range(tt // lc):
            rs = slice(ci * lc, (ci + 1) * lc)
            inter.append(_mm_nt(cm[rs], h))
            st = _mm_tn(xe[rs], bm[rs])
            cdec = jnp.exp(gend[ci * lc:ci * lc + 1, :])
            h = jnp.concatenate([h[e * p:(e + 1) * p] * cdec[:, e:e + 1] + st[e * p:(e + 1) * p]
                                 for e in range(SSD_HPG)], axis=0)
            yield
        h_s[hs, :] = h
        y = y_intra + jnp.concatenate(inter, axis=0) * eg_x + x * dskip_ref[k]
        y = y * _silu(z_ref[0, :, wide])
        y = y * lax.rsqrt(jnp.mean(y * y, axis=-1, keepdims=True) + EPS) * ng_ref[k]
        y_ref[0, :, wide] = y.astype(y_ref.dtype)

    streams = [group(k) for k in range(n_grp)]
    while streams:
        streams = [s for s in streams if next(s, True) is None]

    xx_s[CONV_H0:CONV_PAD, :] = xx_s[tt + CONV_H0:tt + CONV_PAD, :]
    xb_s[CONV_H0:CONV_PAD, :] = xb_s[tt + CONV_H0:tt + CONV_PAD, :]
    xc_s[CONV_H0:CONV_PAD, :] = xc_s[tt + CONV_H0:tt + CONV_PAD, :]

    @pl.when(t == pl.num_programs(2) - 1)
    def _():
        hout_ref[0] = h_s[...].reshape(n_grp * SSD_HPG, p, SSD_DSTATE)


def _ssd_core(proj, conv_hist, h0, conv_w, conv_b, dt_bias, a_log, d_skip, norm_g):
    b, t, _ = proj.shape
    lc = min(CHUNK, t)
    tt = min(ROW_TILE, t)
    g_n = SSD_GROUPS
    gp = SSD_GROUPS_PER_STEP
    gs = SSD_GS
    wide = gp * gs
    narrow = gp * LANES
    xblk = SSD_INNER // wide
    bblk = 2 * SSD_INNER // narrow
    cblk = bblk + g_n // gp
    dblk = cblk + g_n // gp
    hist_b = SSD_INNER // narrow
    hist_c = hist_b + g_n // gp

    def per_group(v):
        return jnp.pad(v.reshape(g_n, 1, SSD_HPG), ((0, 0), (0, 0), (0, LANES - SSD_HPG)))

    dskip_x = jnp.repeat(d_skip, SSD_HEADDIM).reshape(g_n, 1, gs)
    grow = pl.BlockSpec((gp, 1, LANES), lambda i, g, s: (g, 0, 0))
    cwb = conv_b.reshape(1, SSD_CONV_DIM)
    state = pl.BlockSpec((1, gp * SSD_HPG, SSD_HEADDIM, SSD_DSTATE), lambda i, g, s: (i, g, 0, 0))
    in_specs = [
        pl.BlockSpec((1, tt, wide), lambda i, g, s: (i, s, g)),
        pl.BlockSpec((1, tt, wide), lambda i, g, s: (i, s, xblk + g)),
        pl.BlockSpec((1, tt, narrow), lambda i, g, s: (i, s, bblk + g)),
        pl.BlockSpec((1, tt, narrow), lambda i, g, s: (i, s, cblk + g)),
        pl.BlockSpec((1, tt, narrow), lambda i, g, s: (i, s, dblk + g)),
        pl.BlockSpec((1, CONV_W - 1, wide), lambda i, g, s: (i, 0, g)),
        pl.BlockSpec((1, CONV_W - 1, narrow), lambda i, g, s: (i, 0, hist_b + g)),
        pl.BlockSpec((1, CONV_W - 1, narrow), lambda i, g, s: (i, 0, hist_c + g)),
        pl.BlockSpec((CONV_W, wide), lambda i, g, s: (0, g)),
        pl.BlockSpec((CONV_W, narrow), lambda i, g, s: (0, hist_b + g)),
        pl.BlockSpec((CONV_W, narrow), lambda i, g, s: (0, hist_c + g)),
        pl.BlockSpec((1, wide), lambda i, g, s: (0, g)),
        pl.BlockSpec((1, narrow), lambda i, g, s: (0, hist_b + g)),
        pl.BlockSpec((1, narrow), lambda i, g, s: (0, hist_c + g)),
        state,
        grow, grow,
        pl.BlockSpec((gp, 1, gs), lambda i, g, s: (g, 0, 0)),
        pl.BlockSpec((gp, 1, gs), lambda i, g, s: (g, 0, 0)),
    ]
    return pl.pallas_call(
        functools.partial(_ssd_kernel, lc),
        grid=(b, g_n // gp, t // tt),
        in_specs=in_specs,
        out_specs=[pl.BlockSpec((1, tt, wide), lambda i, g, s: (i, s, g)), state],
        out_shape=[jax.ShapeDtypeStruct((b, t, SSD_INNER), BF16),
                   jax.ShapeDtypeStruct((b, SSD_HEADS, SSD_HEADDIM, SSD_DSTATE), F32)],
        scratch_shapes=[pltpu.VMEM((CONV_PAD + tt, wide), F32),
                        pltpu.VMEM((CONV_PAD + tt, narrow), F32),
                        pltpu.VMEM((CONV_PAD + tt, narrow), F32),
                        pltpu.VMEM((gp * SSD_HPG * SSD_HEADDIM, SSD_DSTATE), F32)],
        compiler_params=_cparams(("arbitrary", "arbitrary", "arbitrary")),
        name="ssd_core",
    )(proj, proj, proj, proj, proj, conv_hist, conv_hist, conv_hist, conv_w, conv_w, conv_w,
      cwb, cwb, cwb, h0, per_group(a_log), per_group(dt_bias), dskip_x,
      norm_g.reshape(g_n, 1, gs))


def _outproj_kernel(o_ref, x_ref, g1_ref, w_ref, n2_ref, sc_ref, sh_ref, wr_ref, br_ref,
                    x1_ref, h2_ref, ri_ref, rw_ref, cnt_ref, cnt_s):
    bb, tt, d = x_ref.shape
    tm = bb * tt
    step = pl.program_id(0) * pl.num_programs(1) + pl.program_id(1)

    @pl.when(step == 0)
    def _():
        cnt_s[...] = jnp.zeros_like(cnt_s)

    out = jnp.dot(o_ref[...].reshape(tm, o_ref.shape[-1]), w_ref[...], preferred_element_type=F32)
    x1 = x_ref[...] + g1_ref[...] * out.reshape(bb, tt, d)
    x1_ref[...] = x1
    y = x1 * lax.rsqrt(jnp.mean(x1 * x1, axis=-1, keepdims=True) + EPS) * n2_ref[...]
    h2 = (y * (1.0 + sc_ref[...]) + sh_ref[...]).reshape(tm, d)
    bits = pltpu.bitcast(h2.astype(BF16).astype(F32), jnp.uint32)
    h2_ref[...] = lax.shift_right_logical(bits[:, :d // 2], jnp.uint32(16)) | (bits[:, d // 2:] & jnp.uint32(0xFFFF0000))

    logits = _mm(h2, wr_ref[...]) + br_ref[...]
    lane = lax.broadcasted_iota(I32, (tm, LANES), 1)
    lane_f = lane.astype(F32)
    neg = -jnp.inf
    gl = jnp.where(lane < MOE_GROUPS, logits, neg)
    ge = jnp.exp(gl - jnp.max(gl, axis=-1, keepdims=True))
    grp_p = ge / jnp.sum(ge, axis=-1, keepdims=True)
    gp = jnp.max(grp_p, axis=-1, keepdims=True)
    gi = jnp.min(jnp.where(grp_p == gp, lane_f, float(LANES)), axis=-1, keepdims=True).astype(I32)
    lo = MOE_GROUPS + gi * MOE_PER_GROUP
    emask = (lane >= lo) & (lane < lo + MOE_PER_GROUP)
    sel = jnp.where(emask, logits, neg)
    se = jnp.exp(sel - jnp.max(sel, axis=-1, keepdims=True))
    p = jnp.where(emask, se / jnp.sum(se, axis=-1, keepdims=True), -1.0)
    v1 = jnp.max(p, axis=-1, keepdims=True)
    i1 = jnp.min(jnp.where(p == v1, lane_f, float(LANES)), axis=-1, keepdims=True).astype(I32)
    p2 = jnp.where(lane == i1, -1.0, p)
    v2 = jnp.max(p2, axis=-1, keepdims=True)
    i2 = jnp.min(jnp.where(p2 == v2, lane_f, float(LANES)), axis=-1, keepdims=True).astype(I32)
    den = v1 + v2
    w1 = gp * v1 / den
    w2 = gp * v2 / den
    e1 = i1 - MOE_GROUPS
    e2 = i2 - MOE_GROUPS

    hit1 = lane == e1
    hit2 = lane == e2
    onehot = jnp.where(hit1 | hit2, 1.0, 0.0)
    r = lax.broadcasted_iota(I32, (tm, tm), 0)
    c = lax.broadcasted_iota(I32, (tm, tm), 1)
    before = _mm(jnp.where(r > c, 1.0, 0.0), onehot) + cnt_s[...]
    r1 = jnp.sum(jnp.where(hit1, before, 0.0), axis=-1, keepdims=True).astype(I32)
    r2 = jnp.sum(jnp.where(hit2, before, 0.0), axis=-1, keepdims=True).astype(I32)
    cnt_s[...] = cnt_s[...] + jnp.sum(onehot, axis=0, keepdims=True)
    cnt_ref[...] = cnt_s[...].astype(I32)
    ri_ref[...] = jnp.where(lane == 0, e1, jnp.where(lane == 1, e2, jnp.where(lane == 2, r1, jnp.where(lane == 3, r2, 0))))
    rw_ref[...] = jnp.where(lane == 0, w1, jnp.where(lane == 1, w2, 0.0))


def _out_proj_route(o, x, mod, w_bf16, norm2_g, w_router, b_router):
    b, t, d = x.shape
    kdim = o.shape[-1]
    bb, tt = _tiles(b, t, ROUTE_ROWS)
    tm = bb * tt
    nt = t // tt
    tok = lambda i, j: (i * nt + j, 0)
    modspec = lambda idx: pl.BlockSpec((bb, 1, d), lambda i, j: (i, 0, idx))
    return pl.pallas_call(
        _outproj_kernel,
        grid=(b // bb, nt),
        in_specs=[pl.BlockSpec((bb, tt, kdim), lambda i, j: (i, j, 0)),
                  pl.BlockSpec((bb, tt, d), lambda i, j: (i, j, 0)),
                  modspec(2),
                  pl.BlockSpec((kdim, d), lambda i, j: (0, 0)),
                  pl.BlockSpec((1, d), lambda i, j: (0, 0)),
                  modspec(4), modspec(3),
                  pl.BlockSpec((d, LANES), lambda i, j: (0, 0)),
                  pl.BlockSpec((1, LANES), lambda i, j: (0, 0))],
        out_specs=[pl.BlockSpec((bb, tt, d), lambda i, j: (i, j, 0)),
                   pl.BlockSpec((tm, d // 2), tok),
                   pl.BlockSpec((tm, LANES), tok),
                   pl.BlockSpec((tm, LANES), tok),
                   pl.BlockSpec((1, LANES), lambda i, j: (0, 0))],
        out_shape=[jax.ShapeDtypeStruct((b, t, d), F32),
                   jax.ShapeDtypeStruct((b * t, d // 2), jnp.uint32),
                   jax.ShapeDtypeStruct((b * t, LANES), I32),
                   jax.ShapeDtypeStruct((b * t, LANES), F32),
                   jax.ShapeDtypeStruct((1, LANES), I32)],
        scratch_shapes=[pltpu.VMEM((1, LANES), F32)],
        compiler_params=_cparams(("arbitrary", "arbitrary")),
        name="out_proj_route",
    )(o, x, mod, w_bf16, norm2_g.reshape(1, d), mod, mod, w_router, b_router)


def _dispatch_kernel(dest_ref, h2_ref, xb_in_ref, xb_ref, sem):
    del xb_in_ref
    tm = h2_ref.shape[0]

    def copy(r, k):
        return pltpu.make_async_copy(h2_ref.at[pl.ds(r, 1), :],
                                     xb_ref.at[pl.ds(dest_ref[0, 0, MOE_TOPK * r + k], 1), :], sem)

    for r in range(tm):
        for k in range(MOE_TOPK):
            copy(r, k).start()
    for r in range(tm):
        for k in range(MOE_TOPK):
            copy(r, k).wait()


def _dispatch(h2, dest_tiles, cap):
    n, d = h2.shape
    nt, _, per = dest_tiles.shape
    tm = per // MOE_TOPK
    return pl.pallas_call(
        _dispatch_kernel,
        grid=(nt,),
        in_specs=[pl.BlockSpec((1, 1, per), lambda i: (i, 0, 0), memory_space=pltpu.SMEM),
                  pl.BlockSpec((tm, d), lambda i: (i, 0)),
                  pl.BlockSpec(memory_space=pl.ANY)],
        out_specs=pl.BlockSpec(memory_space=pl.ANY),
        out_shape=jax.ShapeDtypeStruct((cap, d), h2.dtype),
        scratch_shapes=[pltpu.SemaphoreType.DMA],
        input_output_aliases={2: 0},
        compiler_params=_cparams(("arbitrary",)),
        name="moe_dispatch",
    )(dest_tiles, h2, jnp.zeros((cap, d), h2.dtype))


def _expert_kernel(be_ref, nu_ref, x_ref, wg_ref, wu_ref, wd_ref, y_ref):
    used = pl.program_id(0) < nu_ref[0]

    @pl.when(used)
    def _():
        words = x_ref[...]
        x = jnp.concatenate([pltpu.bitcast(lax.shift_left(words, jnp.uint32(16)), F32),
                             pltpu.bitcast(words & jnp.uint32(0xFFFF0000), F32)], axis=-1).astype(BF16)
        hid = _silu(_mm(x, wg_ref[0, 0])) * _mm(x, wu_ref[0, 0])
        y_ref[...] = _mm(hid, wd_ref[0, 0])

    @pl.when(jnp.logical_not(used))
    def _():
        y_ref[...] = jnp.zeros_like(y_ref)


def _experts(xb, rows, block_e, n_used, layer, w_gate, w_up, w_down):
    cap, half = xb.shape
    d = 2 * half
    nb = cap // rows
    last = lambda i, nu: jnp.maximum(jnp.minimum(i, nu[0] - 1), 0)
    blk = lambda i, be, nu: (last(i, nu), 0)
    wsel = lambda i, be, nu: (layer, be[last(i, nu)], 0, 0)
    return pl.pallas_call(
        _expert_kernel,
        grid_spec=pltpu.PrefetchScalarGridSpec(
            num_scalar_prefetch=2,
            grid=(nb,),
            in_specs=[pl.BlockSpec((rows, half), blk),
                      pl.BlockSpec((1, 1, d, D_EXPERT), wsel),
                      pl.BlockSpec((1, 1, d, D_EXPERT), wsel),
                      pl.BlockSpec((1, 1, D_EXPERT, d), wsel)],
            out_specs=pl.BlockSpec((rows, d), lambda i, be, nu: (i, 0))),
        out_shape=jax.ShapeDtypeStruct((cap, d), F32),
        compiler_params=_cparams(("arbitrary",)),
        name="moe_experts",
    )(block_e, n_used, xb, w_gate, w_up, w_down)


def _combine_kernel(final, dest_ref, x1_ref, rw_ref, g2_ref, fg_ref, yb_ref, o_ref, buf_s, sem):
    bb, tt, d = x1_ref.shape
    tm = bb * tt

    def copy(r, k):
        return pltpu.make_async_copy(yb_ref.at[pl.ds(dest_ref[0, 0, MOE_TOPK * r + k], 1), :],
                                     buf_s.at[k, pl.ds(r, 1), :], sem)

    for r in range(tm):
        for k in range(MOE_TOPK):
            copy(r, k).start()
    for r in range(tm):
        for k in range(MOE_TOPK):
            copy(r, k).wait()
    rw = rw_ref[...]
    moe = buf_s[0] * rw[:, 0:1] + buf_s[1] * rw[:, 1:2]
    x2 = x1_ref[...] + g2_ref[...] * moe.reshape(bb, tt, d)
    if final:
        x2 = x2 * lax.rsqrt(jnp.mean(x2 * x2, axis=-1, keepdims=True) + EPS) * fg_ref[...]
    o_ref[...] = x2


def _combine(x1, route_w, mod, yb, dest_tiles, final_g, final):
    b, t, d = x1.shape
    bb, tt = _tiles(b, t, MOE_MOVE_ROWS)
    tm = bb * tt
    nt = t // tt
    per = dest_tiles.shape[-1]
    return pl.pallas_call(
        functools.partial(_combine_kernel, final),
        grid=(b // bb, nt),
        in_specs=[pl.BlockSpec((1, 1, per), lambda i, j: (i * nt + j, 0, 0), memory_space=pltpu.SMEM),
                  pl.BlockSpec((bb, tt, d), lambda i, j: (i, j, 0)),
                  pl.BlockSpec((tm, LANES), lambda i, j: (i * nt + j, 0)),
                  pl.BlockSpec((bb, 1, d), lambda i, j: (i, 0, 5)),
                  pl.BlockSpec((1, d), lambda i, j: (0, 0)),
                  pl.BlockSpec(memory_space=pl.ANY)],
        out_specs=pl.BlockSpec((bb, tt, d), lambda i, j: (i, j, 0)),
        out_shape=jax.ShapeDtypeStruct((b, t, d), F32),
        scratch_shapes=[pltpu.VMEM((MOE_TOPK, tm, d), F32), pltpu.SemaphoreType.DMA],
        compiler_params=_cparams(("arbitrary", "arbitrary")),
        name="moe_combine",
    )(dest_tiles, x1, route_w, mod, final_g.reshape(1, d), yb)


def _moe(layer, x1, h2, route_i, route_w, counts, mod, w_gate, w_up, w_down, final_g, final):
    b, t, d = x1.shape
    n = b * t
    bb, tt = _tiles(b, t, MOE_MOVE_ROWS)
    tm = bb * tt
    n_asg = n * MOE_TOPK
    rows = min(EXPERT_ROWS_MAX, max(EXPERT_ROWS_MIN, pl.next_power_of_2(n_asg // (2 * N_EXPERTS))))
    nb = (n_asg + N_EXPERTS * (rows - 1) + rows - 1) // rows
    cap = nb * rows
    cnt = counts[0, :N_EXPERTS]
    padded = (cnt + rows - 1) // rows * rows
    ends = jnp.cumsum(padded)
    pstart = ends - padded
    eid = route_i[:, :MOE_TOPK]
    rank = route_i[:, MOE_TOPK:2 * MOE_TOPK]
    first_slot = jnp.sum(jnp.where(eid[..., None] == jnp.arange(N_EXPERTS, dtype=I32), pstart, 0), axis=-1)
    dest_tiles = (first_slot + rank).astype(I32).reshape(n // tm, 1, tm * MOE_TOPK)
    blk_start = jnp.arange(nb, dtype=I32) * rows
    block_e = jnp.minimum(jnp.sum(blk_start[:, None] >= ends[None, :], axis=-1), N_EXPERTS - 1).astype(I32)
    n_used = (ends[-1:] // rows).astype(I32)
    xb = _dispatch(h2, dest_tiles, cap)
    yb = _experts(xb, rows, block_e, n_used, layer, w_gate, w_up, w_down)
    return _combine(x1, route_w, mod, yb, dest_tiles, final_g, final)


def _pad_cols(w, n):
    return jnp.pad(w, ((0, 0), (0, n - w.shape[1])))


def _trunk(x, mods, gdn_conv, gdn_ssm, ssd_conv, ssd_ssm, w):
    b, t, d = x.shape
    gdn_main = GDN_CONV_DIM + GDN_VD
    w_in = jnp.concatenate([w['gdn_w_in'][0][:, :gdn_main], _pad_cols(w['gdn_w_in'][0][:, gdn_main:], LANES)],
                           axis=1).astype(BF16)
    proj = _in_proj(x, w['norm1_g'][0], mods[0], 1, 0, w_in)
    o, gdn_state = _gdn_core(proj, gdn_conv[0], gdn_ssm[0], w['gdn_conv_w'][0], w['gdn_A_log'][0],
                             w['gdn_dt_bias'][0], w['gdn_norm_g'][0])
    gdn_hist = proj[:, t - (CONV_W - 1):, :GDN_CONV_DIM]
    x = _layer_tail(0, o, x, mods[0], w['gdn_w_out'][0], w, False)
    ssd_main = SSD_INNER + SSD_CONV_DIM
    dt_cols = [_pad_cols(w['ssd_w_in'][0][:, ssd_main + g * SSD_HPG: ssd_main + (g + 1) * SSD_HPG], LANES)
               for g in range(SSD_GROUPS)]
    w_in = jnp.concatenate([w['ssd_w_in'][0][:, :ssd_main]] + dt_cols, axis=1).astype(BF16)
    proj = _in_proj(x, w['norm1_g'][1], mods[1], 1, 0, w_in)
    y, ssd_state = _ssd_core(proj, ssd_conv[0], ssd_ssm[0], w['ssd_conv_w'][0], w['ssd_conv_b'][0],
                             w['ssd_dt_bias'][0], w['ssd_A_log'][0], w['ssd_D'][0], w['ssd_norm_g'][0])
    ssd_hist = proj[:, t - (CONV_W - 1):, SSD_INNER:ssd_main]
    y_out = _layer_tail(1, y, x, mods[1], w['ssd_w_out'][0], w, True)
    return y_out, gdn_hist[None], gdn_state[None], ssd_hist[None], ssd_state[None]


def _layer_tail(i, mixed, x, mod, w_out, w, final):
    d = x.shape[-1]
    w_router = _pad_cols(jnp.concatenate([w['moe_w_group'][i], w['moe_w_expert'][i]], axis=1), LANES)
    b_router = _pad_cols(jnp.concatenate([w['moe_b_group'][i], w['moe_b_expert'][i]]).reshape(1, -1), LANES)
    x1, h2, route_i, route_w, counts = _out_proj_route(mixed, x, mod, w_out.astype(BF16), w['norm2_g'][i],
                                                       w_router, b_router)
    del d
    return _moe(i, x1, h2, route_i, route_w, counts, mod, w['moe_w_gate'], w['moe_w_up'], w['moe_w_down'],
                w['final_norm_g'], final)


def kernel(x_prompt, x_sample, state_gdn_conv, state_gdn_ssm, state_ssd_conv, state_ssd_ssm, c_prompt, c_sample,
           norm1_g, norm2_g, ada_w, ada_b, gdn_w_in, gdn_conv_w, gdn_A_log, gdn_dt_bias, gdn_norm_g, gdn_w_out,
           ssd_w_in, ssd_conv_w, ssd_conv_b, ssd_dt_bias, ssd_A_log, ssd_D, ssd_norm_g, ssd_w_out,
           moe_w_group, moe_b_group, moe_w_expert, moe_b_expert, moe_w_gate, moe_w_up, moe_w_down, final_norm_g):
    w = {'norm1_g': norm1_g, 'norm2_g': norm2_g, 'gdn_w_in': gdn_w_in, 'gdn_conv_w': gdn_conv_w,
         'gdn_A_log': gdn_A_log, 'gdn_dt_bias': gdn_dt_bias, 'gdn_norm_g': gdn_norm_g, 'gdn_w_out': gdn_w_out,
         'ssd_w_in': ssd_w_in, 'ssd_conv_w': ssd_conv_w, 'ssd_conv_b': ssd_conv_b, 'ssd_dt_bias': ssd_dt_bias,
         'ssd_A_log': ssd_A_log, 'ssd_D': ssd_D, 'ssd_norm_g': ssd_norm_g, 'ssd_w_out': ssd_w_out,
         'moe_w_group': moe_w_group, 'moe_b_group': moe_b_group, 'moe_w_expert': moe_w_expert,
         'moe_b_expert': moe_b_expert, 'moe_w_gate': moe_w_gate, 'moe_w_up': moe_w_up, 'moe_w_down': moe_w_down,
         'final_norm_g': final_norm_g}
    bp = x_prompt.shape[0]
    bs = x_sample.shape[0]
    dt_ = x_prompt.dtype
    n_seq = bp + bs
    n_pad = -n_seq % SUBLANES
    c_all = jnp.concatenate([c_prompt, c_sample, jnp.zeros((n_pad, c_prompt.shape[1]), dt_)], axis=0)
    mod_all = _ada_mod(c_all, ada_w, ada_b)
    mods_p = [mod_all[l, :bp][:, None, :] for l in range(DEPTH)]
    mods_s = [mod_all[l, bp:n_seq][:, None, :] for l in range(DEPTH)]
    n_gdn = state_gdn_conv.shape[0]
    n_ssd = state_ssd_conv.shape[0]
    z_gc = jnp.zeros((n_gdn, bp) + state_gdn_conv.shape[2:], dt_)
    z_gs = jnp.zeros((n_gdn, bp) + state_gdn_ssm.shape[2:], dt_)
    z_sc = jnp.zeros((n_ssd, bp) + state_ssd_conv.shape[2:], dt_)
    z_ss = jnp.zeros((n_ssd, bp) + state_ssd_ssm.shape[2:], dt_)
    y_p, p_gc, p_gs, p_sc, p_ss = _trunk(x_prompt, mods_p, z_gc, z_gs, z_sc, z_ss, w)
    y_s, s_gc, s_gs, s_sc, s_ss = _trunk(x_sample, mods_s, state_gdn_conv, state_gdn_ssm,
                                         state_ssd_conv, state_ssd_ssm, w)
    return (y_p, y_s, p_gc, p_gs, p_sc, p_ss, s_gc, s_gs, s_sc, s_ss)
```
